```python
import math
import jax, jax.numpy as jnp
from jax import lax
import numpy as np

D_MODEL = 1024
BATCH = 1
SEQ = 16384
DEPTH = 4
DEC_BATCH = 4
DEC_SEQ = 4096
PAST_LEN = 128

D_MIX = D_MODEL
CONV_W = D_MODEL // 4
CONV_KERNEL = 31
CONV_PAD = CONV_KERNEL // 2
POOL_W = D_MODEL // 4
POOL_WINDOWS = (2, 4, 8, 16)
N_POOL_GROUPS = len(POOL_WINDOWS)
POOL_GW = POOL_W // N_POOL_GROUPS
ATTN_W = D_MIX - CONV_W - POOL_W
N_HEADS = 4
V_HEAD_DIM = ATTN_W // N_HEADS
QK_NOPE_DIM = 128
QK_ROPE_DIM = 64
QK_HEAD_DIM = QK_NOPE_DIM + QK_ROPE_DIM
Q_LORA_RANK = 256
KV_LORA_RANK = 128
ROPE_BASE = 10000.0
Q_BLOCK = 128
D_IN = 2 * CONV_W + POOL_W + Q_LORA_RANK + KV_LORA_RANK + QK_ROPE_DIM
D_FF = ((8 * D_MODEL + 3 * 256 - 1) // (3 * 256)) * 256
ALPHA = (2.0 * DEPTH) ** 0.25
BETA = (8.0 * DEPTH) ** -0.25
LN_EPS = 1e-5
RMS_EPS = 1e-6

kernel_name = "hybrid_conv_pool_mla_deepnorm_encoder"


def layer_norm(x, g, b):
    xf = x.astype(jnp.float32)
    mu = jnp.mean(xf, axis=-1, keepdims=True)
    xc = xf - mu
    var = jnp.mean(xc * xc, axis=-1, keepdims=True)
    return (xc * lax.rsqrt(var + LN_EPS) * g + b).astype(x.dtype)


def rms_norm(x, g):
    xf = x.astype(jnp.float32)
    ms = jnp.mean(xf * xf, axis=-1, keepdims=True)
    return (xf * lax.rsqrt(ms + RMS_EPS) * g).astype(x.dtype)


def rope_tables(S):
    pos = jnp.arange(S, dtype=jnp.float32)
    inv_freq = 1.0 / (ROPE_BASE ** (jnp.arange(0, QK_ROPE_DIM, 2, dtype=jnp.float32) / QK_ROPE_DIM))
    ang = pos[:, None] * inv_freq[None, :]
    return jnp.cos(ang), jnp.sin(ang)


def apply_rope(x, cos, sin):
    x1, x2 = jnp.split(x, 2, axis=-1)
    out = jnp.concatenate([x1 * cos - x2 * sin, x1 * sin + x2 * cos], axis=-1)
    return out.astype(x.dtype)


def conv_module(u, w_dw, b_dw, g_cn, b_cn, w_pw):
    a, gate = jnp.split(u, 2, axis=-1)
    h = a * jax.nn.sigmoid(gate)
    h = lax.conv_general_dilated(
        h, w_dw[:, None, :].astype(h.dtype), window_strides=(1,), padding=[(CONV_PAD, CONV_PAD)],
        dimension_numbers=("NWC", "WIO", "NWC"), feature_group_count=CONV_W) + b_dw
    h = jax.nn.silu(layer_norm(h, g_cn, b_cn))
    return h @ w_pw


def pool_mixer(p, w_pool, pool_scale):
    B, S, C = p.shape
    pf = p.astype(jnp.float32)
    cs = jnp.concatenate([jnp.zeros((B, 1, C), jnp.float32), jnp.cumsum(pf, axis=1)], axis=1)
    t = jnp.arange(S)
    means = []
    for g, w in enumerate(POOL_WINDOWS):
        left = w // 2
        right = w - 1 - left
        lo = jnp.clip(t - left, 0, S)
        hi = jnp.clip(t + right + 1, 0, S)
        csg = cs[..., g * POOL_GW:(g + 1) * POOL_GW]
        window_sum = jnp.take(csg, hi, axis=1) - jnp.take(csg, lo, axis=1)
        means.append(window_sum / (hi - lo).astype(jnp.float32)[None, :, None])
    d = (jnp.concatenate(means, axis=-1) - pf).astype(p.dtype)
    d = d.reshape(B, S, N_POOL_GROUPS, POOL_GW)
    y = jnp.einsum('bsgc,gcd->bsgd', d, w_pool).reshape(B, S, C)
    return y * pool_scale


def mla(c_q, c_kv, k_rope, g_q, g_kv, w_uq, w_ukv, cos, sin):
    B, S, _ = c_q.shape
    q = (rms_norm(c_q, g_q) @ w_uq).reshape(B, S, N_HEADS, QK_HEAD_DIM)
    q_nope = q[..., :QK_NOPE_DIM]
    q_rope = apply_rope(q[..., QK_NOPE_DIM:], cos[None, :, None, :], sin[None, :, None, :])
    kv = (rms_norm(c_kv, g_kv) @ w_ukv).reshape(B, S, N_HEADS, QK_NOPE_DIM + V_HEAD_DIM)
    k_nope = kv[..., :QK_NOPE_DIM]
    v = kv[..., QK_NOPE_DIM:]
    k_r = apply_rope(k_rope, cos[None], sin[None])
    scale = QK_HEAD_DIM ** -0.5
    nb = S // Q_BLOCK
    qn = q_nope.reshape(B, nb, Q_BLOCK, N_HEADS, QK_NOPE_DIM).transpose(1, 0, 2, 3, 4)
    qr = q_rope.reshape(B, nb, Q_BLOCK, N_HEADS, QK_ROPE_DIM).transpose(1, 0, 2, 3, 4)

    def attend_block(args):
        qn_b, qr_b = args
        s = (jnp.einsum('bqhd,bkhd->bhqk', qn_b, k_nope)
             + jnp.einsum('bqhr,bkr->bhqk', qr_b, k_r))
        pr = jax.nn.softmax(s.astype(jnp.float32) * scale, axis=-1).astype(v.dtype)
        return jnp.einsum('bhqk,bkhd->bqhd', pr, v)

    o = lax.map(attend_block, (qn, qr))
    return o.transpose(1, 0, 2, 3, 4).reshape(B, S, ATTN_W)


def encoder_layer(x, cos, sin, w_in, w_dw, b_dw, g_cn, b_cn, w_pw, w_pool, pool_scale,
                  g_q, g_kv, w_uq, w_ukv, w_out, ln1_g, ln1_b, w_gate, w_up, w_down, ln2_g, ln2_b):
    u = x @ w_in
    o0 = 2 * CONV_W
    o1 = o0 + POOL_W
    o2 = o1 + Q_LORA_RANK
    o3 = o2 + KV_LORA_RANK
    y_conv = conv_module(u[..., :o0], w_dw, b_dw, g_cn, b_cn, w_pw)
    y_pool = pool_mixer(u[..., o0:o1], w_pool, pool_scale)
    y_attn = mla(u[..., o1:o2], u[..., o2:o3], u[..., o3:], g_q, g_kv, w_uq, w_ukv, cos, sin)
    mix = jnp.concatenate([y_conv, y_pool, y_attn], axis=-1) @ w_out
    x = layer_norm(ALPHA * x + mix, ln1_g, ln1_b)
    h = (jax.nn.silu(x @ w_gate) * (x @ w_up)) @ w_down
    return layer_norm(ALPHA * x + h, ln2_g, ln2_b)


def trunk(x, w_in, w_dw, b_dw, g_cn, b_cn, w_pw, w_pool, pool_scale, g_q, g_kv, w_uq, w_ukv,
          w_out, ln1_g, ln1_b, w_gate, w_up, w_down, ln2_g, ln2_b):
    cos, sin = rope_tables(x.shape[1])
    for l in range(DEPTH):
        x = encoder_layer(x, cos, sin, w_in[l], w_dw[l], b_dw[l], g_cn[l], b_cn[l], w_pw[l],
                          w_pool[l], pool_scale[l], g_q[l], g_kv[l], w_uq[l], w_ukv[l], w_out[l],
                          ln1_g[l], ln1_b[l], w_gate[l], w_up[l], w_down[l], ln2_g[l], ln2_b[l])
    return x


def setup_inputs(seed: int = 0) -> dict:
    key = jax.random.key(seed)
    ks = jax.random.split(key, 24)
    f32 = jnp.float32

    def nrm(k, shape, s):
        return jax.random.normal(k, shape, f32) * s

    def gain(k, shape):
        return 1.0 + 0.02 * jax.random.normal(k, shape, f32)

    L = DEPTH
    return {
        "x_prompt": jax.random.normal(ks[0], (BATCH, SEQ, D_MODEL), f32),
        "x_sample": jax.random.normal(ks[1], (DEC_BATCH, DEC_SEQ, D_MODEL), f32),
        "w_in": nrm(ks[2], (L, D_MODEL, D_IN), D_MODEL ** -0.5),
        "w_dw": nrm(ks[3], (L, CONV_KERNEL, CONV_W), CONV_KERNEL ** -0.5),
        "b_dw": nrm(ks[4], (L, CONV_W), 0.01),
        "g_cn": gain(ks[5], (L, CONV_W)),
        "b_cn": nrm(ks[6], (L, CONV_W), 0.01),
        "w_pw": nrm(ks[7], (L, CONV_W, CONV_W), CONV_W ** -0.5),
        "w_pool": nrm(ks[8], (L, N_POOL_GROUPS, POOL_GW, POOL_GW), POOL_GW ** -0.5),
        "pool_scale": 1.0 + 0.1 * jax.random.normal(ks[9], (L, POOL_W), f32),
        "g_q": gain(ks[10], (L, Q_LORA_RANK)),
        "g_kv": gain(ks[11], (L, KV_LORA_RANK)),
        "w_uq": nrm(ks[12], (L, Q_LORA_RANK, N_HEADS * QK_HEAD_DIM), Q_LORA_RANK ** -0.5),
        "w_ukv": nrm(ks[13], (L, KV_LORA_RANK, N_HEADS * (QK_NOPE_DIM + V_HEAD_DIM)), KV_LORA_RANK ** -0.5),
        "w_out": nrm(ks[14], (L, D_MIX, D_MODEL), BETA * D_MIX ** -0.5),
        "ln1_g": gain(ks[15], (L, D_MODEL)),
        "ln1_b": nrm(ks[16], (L, D_MODEL), 0.01),
        "w_gate": nrm(ks[17], (L, D_MODEL, D_FF), D_MODEL ** -0.5),
        "w_up": nrm(ks[18], (L, D_MODEL, D_FF), D_MODEL ** -0.5),
        "w_down": nrm(ks[19], (L, D_FF, D_MODEL), BETA * D_FF ** -0.5),
        "ln2_g": gain(ks[20], (L, D_MODEL)),
        "ln2_b": nrm(ks[21], (L, D_MODEL), 0.01),
    }


def reference(x_prompt, x_sample, w_in, w_dw, b_dw, g_cn, b_cn, w_pw, w_pool, pool_scale,
              g_q, g_kv, w_uq, w_ukv, w_out, ln1_g, ln1_b, w_gate, w_up, w_down, ln2_g, ln2_b):
    y_prompt = trunk(x_prompt, w_in, w_dw, b_dw, g_cn, b_cn, w_pw, w_pool, pool_scale, g_q, g_kv,
                     w_uq, w_ukv, w_out, ln1_g, ln1_b, w_gate, w_up, w_down, ln2_g, ln2_b)
    y_sample = trunk(x_sample, w_in, w_dw, b_dw, g_cn, b_cn, w_pw, w_pool, pool_scale, g_q, g_kv,
                     w_uq, w_ukv, w_out, ln1_g, ln1_b, w_gate, w_up, w_down, ln2_g, ln2_b)
    return (y_prompt, y_sample)
```

```python
import functools

import jax
import jax.numpy as jnp
from jax import lax
from jax.experimental import pallas as pl
from jax.experimental.pallas import tpu as pltpu

F32 = jnp.float32
BF16 = jnp.bfloat16

D_MODEL = 1024
DEPTH = 4
CONV_W = 256
CONV_KERNEL = 31
CONV_PAD = CONV_KERNEL // 2
POOL_W = 256
POOL_WINDOWS = (2, 4, 8, 16)
POOL_GW = POOL_W // len(POOL_WINDOWS)
ATTN_W = 512
N_HEADS = 4
V_HEAD_DIM = 128
QK_NOPE_DIM = 128
QK_ROPE_DIM = 64
QK_HEAD_DIM = QK_NOPE_DIM + QK_ROPE_DIM
Q_LORA_RANK = 256
KV_LORA_RANK = 128
ROPE_BASE = 10000.0
D_IN = 2 * CONV_W + POOL_W + Q_LORA_RANK + KV_LORA_RANK + QK_ROPE_DIM
D_FF = 2816
ALPHA = (2.0 * DEPTH) ** 0.25
LN_EPS = 1e-5
RMS_EPS = 1e-6

LANES = 128
MXU_DIM = 256
D_IN_PAD = 5 * MXU_DIM
QK_PAD = 2 * LANES
HALO = 16
VMEM_LIMIT = 56 * 1024 * 1024

TM_PREP = 512
TM_CONV = 256
CONV_CHUNK = 64
TQ = 512
TK = 512
TM_FFN = 512
FF_CHUNK = 256


def _const_spec(shape):
    nd = len(shape)
    return pl.BlockSpec(shape, lambda *_: (0,) * nd, pipeline_mode=pl.Buffered(1))


def _swap_rope_halves(x):
    lane = lax.broadcasted_iota(jnp.int32, x.shape, 1)
    return jnp.where(lane < QK_ROPE_DIM // 2,
                     pltpu.roll(x, LANES - QK_ROPE_DIM // 2, 1),
                     pltpu.roll(x, QK_ROPE_DIM // 2, 1))


def _prep_kernel(x_ref, w_in_ref, g_q_ref, g_kv_ref, w_uq_ref, w_ukv_ref, cos_ref, sin_ref,
                 uconv_ref, upool_ref, q_ref, k_ref, v_ref):
    xb = x_ref[0].astype(BF16)
    u = jnp.dot(xb, w_in_ref[...], preferred_element_type=F32)
    uconv_ref[0] = u[:, :2 * CONV_W]
    upool_ref[0] = u[:, 2 * CONV_W:2 * CONV_W + POOL_W]
    o_q = 2 * CONV_W + POOL_W
    o_kv = o_q + Q_LORA_RANK
    o_kr = o_kv + KV_LORA_RANK
    c_q = u[:, o_q:o_kv]
    c_kv = u[:, o_kv:o_kr]
    kr = u[:, o_kr:o_kr + LANES]

    ms_q = jnp.mean(c_q * c_q, axis=-1, keepdims=True)
    cqn = (c_q * lax.rsqrt(ms_q + RMS_EPS) * g_q_ref[...]).astype(BF16)
    ms_kv = jnp.mean(c_kv * c_kv, axis=-1, keepdims=True)
    ckvn = (c_kv * lax.rsqrt(ms_kv + RMS_EPS) * g_kv_ref[...]).astype(BF16)
    q = jnp.dot(cqn, w_uq_ref[...], preferred_element_type=F32)
    kv = jnp.dot(ckvn, w_ukv_ref[...], preferred_element_type=F32)

    cosf = cos_ref[...]
    sinf = sin_ref[...]
    kro = (kr * cosf + _swap_rope_halves(kr) * sinf).astype(BF16)
    for h in range(N_HEADS):
        qr = q[:, N_HEADS * LANES + h * LANES:N_HEADS * LANES + (h + 1) * LANES]
        qro = qr * cosf + _swap_rope_halves(qr) * sinf
        q_ref[0, h, :, 0:LANES] = q[:, h * LANES:(h + 1) * LANES].astype(BF16)
        q_ref[0, h, :, LANES:QK_PAD] = qro.astype(BF16)
        k_ref[0, h, :, 0:LANES] = kv[:, h * LANES:(h + 1) * LANES].astype(BF16)
        k_ref[0, h, :, LANES:QK_PAD] = kro
        v_ref[0, h] = kv[:, (N_HEADS + h) * LANES:(N_HEADS + h + 1) * LANES].astype(BF16)


def _prep_call(x, w_in, g_q, g_kv, w_uq, w_ukv, cosf, sinf):
    B, S, _ = x.shape
    tm = TM_PREP
    grid = (B, S // tm)
    return pl.pallas_call(
        _prep_kernel,
        grid=grid,
        in_specs=[
            pl.BlockSpec((1, tm, D_MODEL), lambda b, i: (b, i, 0)),
            _const_spec((D_MODEL, D_IN_PAD)),
            _const_spec((1, Q_LORA_RANK)),
            _const_spec((1, KV_LORA_RANK)),
            _const_spec((Q_LORA_RANK, 2 * N_HEADS * LANES)),
            _const_spec((KV_LORA_RANK, 2 * N_HEADS * LANES)),
            pl.BlockSpec((tm, LANES), lambda b, i: (i, 0)),
            pl.BlockSpec((tm, LANES), lambda b, i: (i, 0)),
        ],
        out_specs=[
            pl.BlockSpec((1, tm, 2 * CONV_W), lambda b, i: (b, i, 0)),
            pl.BlockSpec((1, tm, POOL_W), lambda b, i: (b, i, 0)),
            pl.BlockSpec((1, N_HEADS, tm, QK_PAD), lambda b, i: (b, 0, i, 0)),
            pl.BlockSpec((1, N_HEADS, tm, QK_PAD), lambda b, i: (b, 0, i, 0)),
            pl.BlockSpec((1, N_HEADS, tm, V_HEAD_DIM), lambda b, i: (b, 0, i, 0)),
        ],
        out_shape=[
            jax.ShapeDtypeStruct((B, S, 2 * CONV_W), F32),
            jax.ShapeDtypeStruct((B, S, POOL_W), F32),
            jax.ShapeDtypeStruct((B, N_HEADS, S, QK_PAD), BF16),
            jax.ShapeDtypeStruct((B, N_HEADS, S, QK_PAD), BF16),
            jax.ShapeDtypeStruct((B, N_HEADS, S, V_HEAD_DIM), BF16),
        ],
        compiler_params=pltpu.CompilerParams(
            dimension_semantics=("arbitrary", "arbitrary"), vmem_limit_bytes=VMEM_LIMIT),
        name="prep",
    )(x, w_in, g_q, g_kv, w_uq, w_ukv, cosf, sinf)


def _glu(u):
    return u[:, :CONV_W] * jax.nn.sigmoid(u[:, CONV_W:])


def _convpool_kernel(uc_ref, uc_prev_ref, uc_next_ref, up_ref, up_prev_ref, up_next_ref,
                     w_dw_ref, b_dw_ref, g_cn_ref, b_cn_ref, w_pw_ref, w_pool_ref, pscale_ref,
                     yconv_ref, ypool_ref, hs_ref, ps_ref, hn_ref, *, seq_len):
    tm = uc_ref.shape[1]
    i = pl.program_id(1)
    has_prev = (i > 0).astype(F32)
    has_next = (i < pl.num_programs(1) - 1).astype(F32)

    hs_ref[0:HALO, :] = _glu(uc_prev_ref[0]) * has_prev
    hs_ref[HALO:HALO + tm, :] = _glu(uc_ref[0])
    hs_ref[HALO + tm:2 * HALO + tm, :] = _glu(uc_next_ref[0]) * has_next
    ps_ref[0:HALO, :] = up_prev_ref[0] * has_prev
    ps_ref[HALO:HALO + tm, :] = up_ref[0]
    ps_ref[HALO + tm:2 * HALO + tm, :] = up_next_ref[0] * has_next

    for c in range(tm // CONV_CHUNK):
        base = c * CONV_CHUNK + HALO - CONV_PAD
        acc = jnp.broadcast_to(b_dw_ref[...], (CONV_CHUNK, CONV_W))
        for k in range(CONV_KERNEL):
            acc = acc + hs_ref[base + k:base + k + CONV_CHUNK, :] * w_dw_ref[k:k + 1, :]
        mu = jnp.mean(acc, axis=-1, keepdims=True)
        xc = acc - mu
        var = jnp.mean(xc * xc, axis=-1, keepdims=True)
        hn = xc * lax.rsqrt(var + LN_EPS) * g_cn_ref[...] + b_cn_ref[...]
        hn_ref[c * CONV_CHUNK:(c + 1) * CONV_CHUNK, :] = (hn * jax.nn.sigmoid(hn)).astype(BF16)
    yconv_ref[0] = jnp.dot(hn_ref[...], w_pw_ref[...], preferred_element_type=F32).astype(BF16)

    def shifted(d, lo, hi):
        return ps_ref[HALO + d:HALO + d + tm, lo:hi]

    t = i * tm + lax.broadcasted_iota(jnp.int32, (tm, 1), 0)

    def count(w):
        left = w // 2
        right = w - 1 - left
        return (jnp.minimum(t + right, seq_len - 1) - jnp.maximum(t - left, 0) + 1).astype(F32)

    lane = lax.broadcasted_iota(jnp.int32, (tm, LANES), 1)
    first = lane < POOL_GW
    s2 = shifted(-1, 0, LANES) + shifted(0, 0, LANES)
    s4 = s2 + shifted(-2, 0, LANES) + shifted(1, 0, LANES)
    mean_lo = jnp.where(first, s2 / count(2), s4 / count(4))
    s8 = shifted(-4, LANES, 2 * LANES)
    for d in range(-3, 4):
        s8 = s8 + shifted(d, LANES, 2 * LANES)
    s16 = s8
    for d in (-8, -7, -6, -5, 4, 5, 6, 7):
        s16 = s16 + shifted(d, LANES, 2 * LANES)
    mean_hi = jnp.where(first, s8 / count(8), s16 / count(16))
    d_lo = (mean_lo - shifted(0, 0, LANES)).astype(BF16)
    d_hi = (mean_hi - shifted(0, LANES, 2 * LANES)).astype(BF16)
    d = jnp.concatenate([d_lo, d_hi], axis=-1)
    y = jnp.dot(d, w_pool_ref[...], preferred_element_type=F32) * pscale_ref[...]
    ypool_ref[0] = y.astype(BF16)


def _convpool_call(uconv, upool, w_dw, b_dw, g_cn, b_cn, w_pw, w_pool_bd, pool_scale):
    B, S, _ = uconv.shape
    tm = TM_CONV
    r = tm // HALO
    n_halo_blocks = S // HALO

    def cur(b, i):
        return (b, i, 0)

    def prev(b, i):
        return (b, jnp.maximum(i * r - 1, 0), 0)

    def nxt(b, i):
        return (b, jnp.minimum((i + 1) * r, n_halo_blocks - 1), 0)

    return pl.pallas_call(
        functools.partial(_convpool_kernel, seq_len=S),
        grid=(B, S // tm),
        in_specs=[
            pl.BlockSpec((1, tm, 2 * CONV_W), cur),
            pl.BlockSpec((1, HALO, 2 * CONV_W), prev),
            pl.BlockSpec((1, HALO, 2 * CONV_W), nxt),
            pl.BlockSpec((1, tm, POOL_W), cur),
            pl.BlockSpec((1, HALO, POOL_W), prev),
            pl.BlockSpec((1, HALO, POOL_W), nxt),
            _const_spec((CONV_KERNEL, CONV_W)),
            _const_spec((1, CONV_W)),
            _const_spec((1, CONV_W)),
            _const_spec((1, CONV_W)),
            _const_spec((CONV_W, CONV_W)),
            _const_spec((POOL_W, POOL_W)),
            _const_spec((1, POOL_W)),
        ],
        out_specs=[
            pl.BlockSpec((1, tm, CONV_W), cur),
            pl.BlockSpec((1, tm, POOL_W), cur),
        ],
        out_shape=[
            jax.ShapeDtypeStruct((B, S, CONV_W), BF16),
            jax.ShapeDtypeStruct((B, S, POOL_W), BF16),
        ],
        scratch_shapes=[
            pltpu.VMEM((tm + 2 * HALO, CONV_W), F32),
            pltpu.VMEM((tm + 2 * HALO, POOL_W), F32),
            pltpu.VMEM((tm, CONV_W), BF16),
        ],
        compiler_params=pltpu.CompilerParams(
            dimension_semantics=("arbitrary", "arbitrary"), vmem_limit_bytes=VMEM_LIMIT),
        name="convpool",
    )(uconv, uconv, uconv, upool, upool, upool, w_dw, b_dw, g_cn, b_cn, w_pw, w_pool_bd, pool_scale)


def _attn_kernel(q_ref, k_ref, v_ref, o_ref, *, seq_len):
    tq = q_ref.shape[2]
    q = q_ref[0, 0]
    scale = QK_HEAD_DIM ** -0.5

    def body(j, carry):
        m, l, acc = carry
        start = pl.multiple_of(j * TK, TK)
        kc = k_ref[0, 0, pl.ds(start, TK), :]
        vc = v_ref[0, 0, pl.ds(start, TK), :]
        s = lax.dot_general(q, kc, (((1,), (1,)), ((), ())), preferred_element_type=F32) * scale
        m_new = jnp.maximum(m, jnp.max(s, axis=-1, keepdims=True))
        alpha = jnp.exp(m - m_new)
        p = jnp.exp(s - m_new)
        l = alpha * l + jnp.sum(p, axis=-1, keepdims=True)
        acc = alpha * acc + jnp.dot(p.astype(BF16), vc, preferred_element_type=F32)
        return m_new, l, acc

    m0 = jnp.full((tq, 1), -jnp.inf, F32)
    l0 = jnp.zeros((tq, 1), F32)
    acc0 = jnp.zeros((tq, V_HEAD_DIM), F32)
    _, l, acc = lax.fori_loop(0, seq_len // TK, body, (m0, l0, acc0))
    o_ref[0] = (acc / l).astype(BF16)


def _attn_call(q, k, v):
    B, H, S, _ = q.shape
    tq = TQ
    return pl.pallas_call(
        functools.partial(_attn_kernel, seq_len=S),
        grid=(B, H, S // tq),
        in_specs=[
            pl.BlockSpec((1, 1, tq, QK_PAD), lambda b, h, i: (b, h, i, 0)),
            pl.BlockSpec((1, 1, S, QK_PAD), lambda b, h, i: (b, h, 0, 0)),
            pl.BlockSpec((1, 1, S, V_HEAD_DIM), lambda b, h, i: (b, h, 0, 0)),
        ],
        out_specs=pl.BlockSpec((1, tq, V_HEAD_DIM), lambda b, h, i: (b, i, h)),
        out_shape=jax.ShapeDtypeStruct((B, S, ATTN_W), BF16),
        compiler_params=pltpu.CompilerParams(
            dimension_semantics=("arbitrary", "arbitrary", "arbitrary"),
            vmem_limit_bytes=VMEM_LIMIT),
        name="attn",
    )(q, k, v)


def _layer_norm(x, g, b):
    mu = jnp.mean(x, axis=-1, keepdims=True)
    xc = x - mu
    var = jnp.mean(xc * xc, axis=-1, keepdims=True)
    return xc * lax.rsqrt(var + LN_EPS) * g + b


def _outffn_kernel(x_ref, yc_ref, yp_ref, ya_ref, w_out_ref, ln1_g_ref, ln1_b_ref,
                   w_gate_ref, w_up_ref, w_down_ref, ln2_g_ref, ln2_b_ref, o_ref, h_ref):
    mix = jnp.dot(yc_ref[...], w_out_ref[0:CONV_W, :], preferred_element_type=F32)
    mix += jnp.dot(yp_ref[...], w_out_ref[CONV_W:CONV_W + POOL_W, :], preferred_element_type=F32)
    mix += jnp.dot(ya_ref[...], w_out_ref[CONV_W + POOL_W:, :], preferred_element_type=F32)
    x1 = _layer_norm(ALPHA * x_ref[...] + mix, ln1_g_ref[...], ln1_b_ref[...])
    x1b = x1.astype(BF16)
    for c in range(D_FF // FF_CHUNK):
        cols = slice(c * FF_CHUNK, (c + 1) * FF_CHUNK)
        g = jnp.dot(x1b, w_gate_ref[:, cols], preferred_element_type=F32)
        u = jnp.dot(x1b, w_up_ref[:, cols], preferred_element_type=F32)
        h_ref[:, cols] = (g * jax.nn.sigmoid(g) * u).astype(BF16)
    y = jnp.dot(h_ref[...], w_down_ref[...], preferred_element_type=F32)
    o_ref[...] = _layer_norm(ALPHA * x1 + y, ln2_g_ref[...], ln2_b_ref[...])


def _outffn_call(x, yc, yp, ya, w_out, ln1_g, ln1_b, w_gate, w_up, w_down, ln2_g, ln2_b):
    T = x.shape[0]
    tm = TM_FFN

    def row(i):
        return (i, 0)

    return pl.pallas_call(
        _outffn_kernel,
        grid=(T // tm,),
        in_specs=[
            pl.BlockSpec((tm, D_MODEL), row),
            pl.BlockSpec((tm, CONV_W), row),
            pl.BlockSpec((tm, POOL_W), row),
            pl.BlockSpec((tm, ATTN_W), row),
            _const_spec((D_MODEL, D_MODEL)),
            _const_spec((1, D_MODEL)),
            _const_spec((1, D_MODEL)),
            _const_spec((D_MODEL, D_FF)),
            _const_spec((D_MODEL, D_FF)),
            _const_spec((D_FF, D_MODEL)),
            _const_spec((1, D_MODEL)),
            _const_spec((1, D_MODEL)),
        ],
        out_specs=pl.BlockSpec((tm, D_MODEL), row),
        out_shape=jax.ShapeDtypeStruct((T, D_MODEL), F32),
        scratch_shapes=[pltpu.VMEM((tm, D_FF), BF16)],
        compiler_params=pltpu.CompilerParams(
            dimension_semantics=("arbitrary",), vmem_limit_bytes=VMEM_LIMIT),
        name="outffn",
    )(x, yc, yp, ya, w_out, ln1_g, ln1_b, w_gate, w_up, w_down, ln2_g, ln2_b)


def _rope_tables(S):
    pos = jnp.arange(S, dtype=F32)
    inv_freq = 1.0 / (ROPE_BASE ** (jnp.arange(0, QK_ROPE_DIM, 2, dtype=F32) / QK_ROPE_DIM))
    ang = pos[:, None] * inv_freq[None, :]
    cos, sin = jnp.cos(ang), jnp.sin(ang)
    zeros = jnp.zeros((S, LANES - QK_ROPE_DIM), F32)
    cosf = jnp.concatenate([cos, cos, zeros], axis=-1)
    sinf = jnp.concatenate([-sin, sin, zeros], axis=-1)
    return cosf, sinf


def _prepare_layer(w_in, w_pw, w_pool, w_uq, w_ukv, w_out, w_gate, w_up, w_down):
    w_in_p = jnp.pad(w_in, ((0, 0), (0, D_IN_PAD - D_IN))).astype(BF16)
    wq = w_uq.reshape(Q_LORA_RANK, N_HEADS, QK_HEAD_DIM)
    wq_nope = wq[:, :, :QK_NOPE_DIM].reshape(Q_LORA_RANK, N_HEADS * QK_NOPE_DIM)
    wq_rope = jnp.pad(wq[:, :, QK_NOPE_DIM:], ((0, 0), (0, 0), (0, LANES - QK_ROPE_DIM)))
    w_uq_p = jnp.concatenate([wq_nope, wq_rope.reshape(Q_LORA_RANK, N_HEADS * LANES)], axis=-1).astype(BF16)
    wkv = w_ukv.reshape(KV_LORA_RANK, N_HEADS, QK_NOPE_DIM + V_HEAD_DIM)
    w_ukv_p = jnp.concatenate(
        [wkv[:, :, :QK_NOPE_DIM].reshape(KV_LORA_RANK, N_HEADS * QK_NOPE_DIM),
         wkv[:, :, QK_NOPE_DIM:].reshape(KV_LORA_RANK, N_HEADS * V_HEAD_DIM)], axis=-1).astype(BF16)
    w_pool_bd = jnp.zeros((POOL_W, POOL_W), F32)
    for g in range(len(POOL_WINDOWS)):
        w_pool_bd = w_pool_bd.at[g * POOL_GW:(g + 1) * POOL_GW, g * POOL_GW:(g + 1) * POOL_GW].set(w_pool[g])
    return (w_in_p, w_pw.astype(BF16), w_pool_bd.astype(BF16), w_uq_p, w_ukv_p,
            w_out.astype(BF16), w_gate.astype(BF16), w_up.astype(BF16), w_down.astype(BF16))


def _trunk(x, layers, small):
    B, S, _ = x.shape
    cosf, sinf = _rope_tables(S)
    for l in range(DEPTH):
        w_in_p, w_pw_b, w_pool_bd, w_uq_p, w_ukv_p, w_out_b, w_gate_b, w_up_b, w_down_b = layers[l]
        (w_dw, b_dw, g_cn, b_cn, pool_scale, g_q, g_kv, ln1_g, ln1_b, ln2_g, ln2_b) = small[l]
        uconv, upool, q, k, v = _prep_call(x, w_in_p, g_q, g_kv, w_uq_p, w_ukv_p, cosf, sinf)
        yconv, ypool = _convpool_call(uconv, upool, w_dw, b_dw, g_cn, b_cn, w_pw_b, w_pool_bd, pool_scale)
        yattn = _attn_call(q, k, v)
        x = _outffn_call(
            x.reshape(B * S, D_MODEL), yconv.reshape(B * S, CONV_W), ypool.reshape(B * S, POOL_W),
            yattn.reshape(B * S, ATTN_W), w_out_b, ln1_g, ln1_b, w_gate_b, w_up_b, w_down_b,
            ln2_g, ln2_b).reshape(B, S, D_MODEL)
    return x


def kernel(x_prompt, x_sample, w_in, w_dw, b_dw, g_cn, b_cn, w_pw, w_pool, pool_scale, g_q, g_kv,
           w_uq, w_ukv, w_out, ln1_g, ln1_b, w_gate, w_up, w_down, ln2_g, ln2_b):
    layers = [_prepare_layer(w_in[l], w_pw[l], w_pool[l], w_uq[l], w_ukv[l], w_out[l],
                             w_gate[l], w_up[l], w_down[l]) for l in range(DEPTH)]

    def row(a, l):
        return a[l].reshape(1, -1)

    small = [(w_dw[l], row(b_dw, l), row(g_cn, l), row(b_cn, l), row(pool_scale, l), row(g_q, l),
              row(g_kv, l), row(ln1_g, l), row(ln1_b, l), row(ln2_g, l), row(ln2_b, l))
             for l in range(DEPTH)]
    y_prompt = _trunk(x_prompt, layers, small)
    y_sample = _trunk(x_sample, layers, small)
    return (y_prompt, y_sample)
```

```python
import functools

import jax
import jax.numpy as jnp
from jax import lax
from jax.experimental import pallas as pl
from jax.experimental.pallas import tpu as pltpu

F32 = jnp.float32
BF16 = jnp.bfloat16

D_MODEL = 1024
DEPTH = 4
CONV_W = 256
CONV_KERNEL = 31
CONV_PAD = CONV_KERNEL // 2
POOL_W = 256
POOL_WINDOWS = (2, 4, 8, 16)
POOL_GW = POOL_W // len(POOL_WINDOWS)
ATTN_W = 512
N_HEADS = 4
V_HEAD_DIM = 128
QK_NOPE_DIM = 128
QK_ROPE_DIM = 64
QK_HEAD_DIM = QK_NOPE_DIM + QK_ROPE_DIM
Q_LORA_RANK = 256
KV_LORA_RANK = 128
ROPE_BASE = 10000.0
D_IN = 2 * CONV_W + POOL_W + Q_LORA_RANK + KV_LORA_RANK + QK_ROPE_DIM
D_FF = 2816
ALPHA = (2.0 * DEPTH) ** 0.25
LN_EPS = 1e-5
RMS_EPS = 1e-6
QK_SCALE_LOG2E = QK_HEAD_DIM ** -0.5 * 1.4426950408889634

LANES = 128
MXU_DIM = 256
D_IN_PAD = 5 * MXU_DIM
QK_PAD = 2 * LANES
HALO = 16
VMEM_LIMIT = 56 * 1024 * 1024

TM_PREP = 512
TM_CONV = 256
CONV_CHUNK = 64
TQ = 512
TK = 512
ATTN_MAX_UNROLL = 5
TM_FFN = 512
FF_CHUNK = 256


def _const_spec(shape):
    nd = len(shape)
    return pl.BlockSpec(shape, lambda *_: (0,) * nd, pipeline_mode=pl.Buffered(1))


def _swap_rope_halves(x):
    lane = lax.broadcasted_iota(jnp.int32, x.shape, 1)
    return jnp.where(lane < QK_ROPE_DIM // 2,
                     pltpu.roll(x, LANES - QK_ROPE_DIM // 2, 1),
                     pltpu.roll(x, QK_ROPE_DIM // 2, 1))


def _prep_kernel(x_ref, w_in_ref, g_q_ref, g_kv_ref, w_uq_ref, w_ukv_ref, cos_ref, sin_ref,
                 uconv_ref, upool_ref, qT_ref, k_ref, vT_ref):
    xb = x_ref[0].astype(BF16)
    u = jnp.dot(xb, w_in_ref[...], preferred_element_type=F32)
    uconv_ref[0] = u[:, :2 * CONV_W]
    upool_ref[0] = u[:, 2 * CONV_W:2 * CONV_W + POOL_W]
    o_q = 2 * CONV_W + POOL_W
    o_kv = o_q + Q_LORA_RANK
    o_kr = o_kv + KV_LORA_RANK
    c_q = u[:, o_q:o_kv]
    c_kv = u[:, o_kv:o_kr]
    kr = u[:, o_kr:o_kr + LANES]

    ms_q = jnp.mean(c_q * c_q, axis=-1, keepdims=True)
    cqn = (c_q * lax.rsqrt(ms_q + RMS_EPS) * g_q_ref[...]).astype(BF16)
    ms_kv = jnp.mean(c_kv * c_kv, axis=-1, keepdims=True)
    ckvn = (c_kv * lax.rsqrt(ms_kv + RMS_EPS) * g_kv_ref[...]).astype(BF16)
    q = jnp.dot(cqn, w_uq_ref[...], preferred_element_type=F32)
    kv = jnp.dot(ckvn, w_ukv_ref[...], preferred_element_type=F32)

    cosf = cos_ref[...]
    sinf = sin_ref[...]
    kro = (kr * cosf + _swap_rope_halves(kr) * sinf).astype(BF16)
    for h in range(N_HEADS):
        qr = q[:, N_HEADS * LANES + h * LANES:N_HEADS * LANES + (h + 1) * LANES]
        qro = qr * cosf + _swap_rope_halves(qr) * sinf
        qT_ref[0, h, 0:LANES, :] = (q[:, h * LANES:(h + 1) * LANES] * QK_SCALE_LOG2E).T.astype(BF16)
        qT_ref[0, h, LANES:QK_PAD, :] = (qro * QK_SCALE_LOG2E).T.astype(BF16)
        k_ref[0, h, :, 0:LANES] = kv[:, h * LANES:(h + 1) * LANES].astype(BF16)
        k_ref[0, h, :, LANES:QK_PAD] = kro
        vT_ref[0, h, 0] = kv[:, (N_HEADS + h) * LANES:(N_HEADS + h + 1) * LANES].T.astype(BF16)


def _prep_call(x, w_in, g_q, g_kv, w_uq, w_ukv, cosf, sinf):
    B, S, _ = x.shape
    tm = TM_PREP
    assert tm == TK
    grid = (B, S // tm)
    return pl.pallas_call(
        _prep_kernel,
        grid=grid,
        in_specs=[
            pl.BlockSpec((1, tm, D_MODEL), lambda b, i: (b, i, 0)),
            _const_spec((D_MODEL, D_IN_PAD)),
            _const_spec((1, Q_LORA_RANK)),
            _const_spec((1, KV_LORA_RANK)),
            _const_spec((Q_LORA_RANK, 2 * N_HEADS * LANES)),
            _const_spec((KV_LORA_RANK, 2 * N_HEADS * LANES)),
            pl.BlockSpec((tm, LANES), lambda b, i: (i, 0)),
            pl.BlockSpec((tm, LANES), lambda b, i: (i, 0)),
        ],
        out_specs=[
            pl.BlockSpec((1, tm, 2 * CONV_W), lambda b, i: (b, i, 0)),
            pl.BlockSpec((1, tm, POOL_W), lambda b, i: (b, i, 0)),
            pl.BlockSpec((1, N_HEADS, QK_PAD, tm), lambda b, i: (b, 0, 0, i)),
            pl.BlockSpec((1, N_HEADS, tm, QK_PAD), lambda b, i: (b, 0, i, 0)),
            pl.BlockSpec((1, N_HEADS, 1, V_HEAD_DIM, tm), lambda b, i: (b, 0, i, 0, 0)),
        ],
        out_shape=[
            jax.ShapeDtypeStruct((B, S, 2 * CONV_W), F32),
            jax.ShapeDtypeStruct((B, S, POOL_W), F32),
            jax.ShapeDtypeStruct((B, N_HEADS, QK_PAD, S), BF16),
            jax.ShapeDtypeStruct((B, N_HEADS, S, QK_PAD), BF16),
            jax.ShapeDtypeStruct((B, N_HEADS, S // tm, V_HEAD_DIM, tm), BF16),
        ],
        compiler_params=pltpu.CompilerParams(
            dimension_semantics=("arbitrary", "arbitrary"), vmem_limit_bytes=VMEM_LIMIT),
        name="prep",
    )(x, w_in, g_q, g_kv, w_uq, w_ukv, cosf, sinf)


def _glu(u):
    return u[:, :CONV_W] * jax.nn.sigmoid(u[:, CONV_W:])


def _convpool_kernel(uc_ref, uc_prev_ref, uc_next_ref, up_ref, up_prev_ref, up_next_ref,
                     w_dw_ref, b_dw_ref, g_cn_ref, b_cn_ref, w_pw_ref, w_pool_ref, pscale_ref,
                     yconv_ref, ypool_ref, hs_ref, ps_ref, hn_ref, *, seq_len):
    tm = uc_ref.shape[1]
    i = pl.program_id(1)
    has_prev = (i > 0).astype(F32)
    has_next = (i < pl.num_programs(1) - 1).astype(F32)

    hs_ref[0:HALO, :] = _glu(uc_prev_ref[0]) * has_prev
    hs_ref[HALO:HALO + tm, :] = _glu(uc_ref[0])
    hs_ref[HALO + tm:2 * HALO + tm, :] = _glu(uc_next_ref[0]) * has_next
    ps_ref[0:HALO, :] = up_prev_ref[0] * has_prev
    ps_ref[HALO:HALO + tm, :] = up_ref[0]
    ps_ref[HALO + tm:2 * HALO + tm, :] = up_next_ref[0] * has_next

    for c in range(tm // CONV_CHUNK):
        base = c * CONV_CHUNK + HALO - CONV_PAD
        acc = jnp.broadcast_to(b_dw_ref[...], (CONV_CHUNK, CONV_W))
        for k in range(CONV_KERNEL):
            acc = acc + hs_ref[base + k:base + k + CONV_CHUNK, :] * w_dw_ref[k:k + 1, :]
        mu = jnp.mean(acc, axis=-1, keepdims=True)
        xc = acc - mu
        var = jnp.mean(xc * xc, axis=-1, keepdims=True)
        hn = xc * lax.rsqrt(var + LN_EPS) * g_cn_ref[...] + b_cn_ref[...]
        hn_ref[c * CONV_CHUNK:(c + 1) * CONV_CHUNK, :] = (hn * jax.nn.sigmoid(hn)).astype(BF16)
    yconv_ref[0] = jnp.dot(hn_ref[...], w_pw_ref[...], preferred_element_type=F32).astype(BF16)

    def shifted(d, lo, hi):
        return ps_ref[HALO + d:HALO + d + tm, lo:hi]

    t = i * tm + lax.broadcasted_iota(jnp.int32, (tm, 1), 0)

    def count(w):
        left = w // 2
        right = w - 1 - left
        return (jnp.minimum(t + right, seq_len - 1) - jnp.maximum(t - left, 0) + 1).astype(F32)

    lane = lax.broadcasted_iota(jnp.int32, (tm, LANES), 1)
    first = lane < POOL_GW
    s2 = shifted(-1, 0, LANES) + shifted(0, 0, LANES)
    s4 = s2 + shifted(-2, 0, LANES) + shifted(1, 0, LANES)
    mean_lo = jnp.where(first, s2 / count(2), s4 / count(4))
    s8 = shifted(-4, LANES, 2 * LANES)
    for d in range(-3, 4):
        s8 = s8 + shifted(d, LANES, 2 * LANES)
    s16 = s8
    for d in (-8, -7, -6, -5, 4, 5, 6, 7):
        s16 = s16 + shifted(d, LANES, 2 * LANES)
    mean_hi = jnp.where(first, s8 / count(8), s16 / count(16))
    d_lo = (mean_lo - shifted(0, 0, LANES)).astype(BF16)
    d_hi = (mean_hi - shifted(0, LANES, 2 * LANES)).astype(BF16)
    d = jnp.concatenate([d_lo, d_hi], axis=-1)
    y = jnp.dot(d, w_pool_ref[...], preferred_element_type=F32) * pscale_ref[...]
    ypool_ref[0] = y.astype(BF16)


def _convpool_call(uconv, upool, w_dw, b_dw, g_cn, b_cn, w_pw, w_pool_bd, pool_scale):
    B, S, _ = uconv.shape
    tm = TM_CONV
    r = tm // HALO
    n_halo_blocks = S // HALO

    def cur(b, i):
        return (b, i, 0)

    def prev(b, i):
        return (b, jnp.maximum(i * r - 1, 0), 0)

    def nxt(b, i):
        return (b, jnp.minimum((i + 1) * r, n_halo_blocks - 1), 0)

    return pl.pallas_call(
        functools.partial(_convpool_kernel, seq_len=S),
        grid=(B, S // tm),
        in_specs=[
            pl.BlockSpec((1, tm, 2 * CONV_W), cur),
            pl.BlockSpec((1, HALO, 2 * CONV_W), prev),
            pl.BlockSpec((1, HALO, 2 * CONV_W), nxt),
            pl.BlockSpec((1, tm, POOL_W), cur),
            pl.BlockSpec((1, HALO, POOL_W), prev),
            pl.BlockSpec((1, HALO, POOL_W), nxt),
            _const_spec((CONV_KERNEL, CONV_W)),
            _const_spec((1, CONV_W)),
            _const_spec((1, CONV_W)),
            _const_spec((1, CONV_W)),
            _const_spec((CONV_W, CONV_W)),
            _const_spec((POOL_W, POOL_W)),
            _const_spec((1, POOL_W)),
        ],
        out_specs=[
            pl.BlockSpec((1, tm, CONV_W), cur),
            pl.BlockSpec((1, tm, POOL_W), cur),
        ],
        out_shape=[
            jax.ShapeDtypeStruct((B, S, CONV_W), BF16),
            jax.ShapeDtypeStruct((B, S, POOL_W), BF16),
        ],
        scratch_shapes=[
            pltpu.VMEM((tm + 2 * HALO, CONV_W), F32),
            pltpu.VMEM((tm + 2 * HALO, POOL_W), F32),
            pltpu.VMEM((tm, CONV_W), BF16),
        ],
        compiler_params=pltpu.CompilerParams(
            dimension_semantics=("arbitrary", "arbitrary"), vmem_limit_bytes=VMEM_LIMIT),
        name="convpool",
    )(uconv, uconv, uconv, upool, upool, upool, w_dw, b_dw, g_cn, b_cn, w_pw, w_pool_bd, pool_scale)


def _attn_kernel(qT_ref, k_ref, vT_ref, o_ref, sa_ref, sb_ref, acc_ref, *, seq_len):
    tq = qT_ref.shape[3]
    nk = seq_len // TK
    qT = qT_ref[0, 0]

    def scores(j, s_ref):
        start = pl.multiple_of(j * TK, TK)
        s = jnp.dot(k_ref[0, 0, pl.ds(start, TK), :], qT, preferred_element_type=F32)
        s_ref[...] = s
        return jnp.max(s, axis=0, keepdims=True)

    def update(j, s_ref, cmax, m, l):
        m_new = jnp.maximum(m, cmax)
        alpha = jnp.exp2(m - m_new)
        p = jnp.exp2(s_ref[...] - m_new)
        l = alpha * l + jnp.sum(p, axis=0, keepdims=True)
        pv = jnp.dot(vT_ref[0, 0, j], p.astype(BF16), preferred_element_type=F32)
        acc_ref[...] = acc_ref[...] * alpha + pv
        return m_new, l

    acc_ref[...] = jnp.zeros_like(acc_ref)
    m = jnp.full((1, tq), -jnp.inf, F32)
    l = jnp.zeros((1, tq), F32)
    cmax_a = scores(0, sa_ref)

    def body(i, carry):
        m, l, cmax_a = carry
        cmax_b = scores(2 * i + 1, sb_ref)
        m, l = update(2 * i, sa_ref, cmax_a, m, l)
        cmax_a = scores(2 * i + 2, sa_ref)
        m, l = update(2 * i + 1, sb_ref, cmax_b, m, l)
        return m, l, cmax_a

    trips = nk // 2 - 1
    unroll = max(u for u in range(1, ATTN_MAX_UNROLL + 1) if trips % u == 0)
    m, l, cmax_a = lax.fori_loop(0, trips, body, (m, l, cmax_a), unroll=unroll)
    cmax_b = scores(nk - 1, sb_ref)
    m, l = update(nk - 2, sa_ref, cmax_a, m, l)
    m, l = update(nk - 1, sb_ref, cmax_b, m, l)
    o_ref[0] = (acc_ref[...] / l).T.astype(BF16)


def _attn_call(qT, k, vT):
    B, H, S, _ = k.shape
    tq = TQ
    assert (S // TK) % 2 == 0
    return pl.pallas_call(
        functools.partial(_attn_kernel, seq_len=S),
        grid=(B, H, S // tq),
        in_specs=[
            pl.BlockSpec((1, 1, QK_PAD, tq), lambda b, h, i: (b, h, 0, i)),
            pl.BlockSpec((1, 1, S, QK_PAD), lambda b, h, i: (b, h, 0, 0)),
            pl.BlockSpec((1, 1, S // TK, V_HEAD_DIM, TK), lambda b, h, i: (b, h, 0, 0, 0)),
        ],
        out_specs=pl.BlockSpec((1, tq, V_HEAD_DIM), lambda b, h, i: (b, i, h)),
        out_shape=jax.ShapeDtypeStruct((B, S, ATTN_W), BF16),
        scratch_shapes=[
            pltpu.VMEM((TK, tq), F32),
            pltpu.VMEM((TK, tq), F32),
            pltpu.VMEM((V_HEAD_DIM, tq), F32),
        ],
        compiler_params=pltpu.CompilerParams(
            dimension_semantics=("arbitrary", "arbitrary", "arbitrary"),
            vmem_limit_bytes=VMEM_LIMIT),
        name="attn",
    )(qT, k, vT)


def _layer_norm(x, g, b):
    mu = jnp.mean(x, axis=-1, keepdims=True)
    xc = x - mu
    var = jnp.mean(xc * xc, axis=-1, keepdims=True)
    return xc * lax.rsqrt(var + LN_EPS) * g + b


def _outffn_kernel(x_ref, yc_ref, yp_ref, ya_ref, w_out_ref, ln1_g_ref, ln1_b_ref,
                   w_gate_ref, w_up_ref, w_down_ref, ln2_g_ref, ln2_b_ref, o_ref, h_ref):
    mix = jnp.dot(yc_ref[...], w_out_ref[0:CONV_W, :], preferred_element_type=F32)
    mix += jnp.dot(yp_ref[...], w_out_ref[CONV_W:CONV_W + POOL_W, :], preferred_element_type=F32)
    mix += jnp.dot(ya_ref[...], w_out_ref[CONV_W + POOL_W:, :], preferred_element_type=F32)
    x1 = _layer_norm(ALPHA * x_ref[...] + mix, ln1_g_ref[...], ln1_b_ref[...])
    x1b = x1.astype(BF16)
    for c in range(D_FF // FF_CHUNK):
        cols = slice(c * FF_CHUNK, (c + 1) * FF_CHUNK)
        g = jnp.dot(x1b, w_gate_ref[:, cols], preferred_element_type=F32)
        u = jnp.dot(x1b, w_up_ref[:, cols], preferred_element_type=F32)
        h_ref[:, cols] = (g * jax.nn.sigmoid(g) * u).astype(BF16)
    y = jnp.dot(h_ref[...], w_down_ref[...], preferred_element_type=F32)
    o_ref[...] = _layer_norm(ALPHA * x1 + y, ln2_g_ref[...], ln2_b_ref[...])


def _outffn_call(x, yc, yp, ya, w_out, ln1_g, ln1_b, w_gate, w_up, w_down, ln2_g, ln2_b):
    T = x.shape[0]
    tm = TM_FFN

    def row(i):
        return (i, 0)

    return pl.pallas_call(
        _outffn_kernel,
        grid=(T // tm,),
        in_specs=[
            pl.BlockSpec((tm, D_MODEL), row),
            pl.BlockSpec((tm, CONV_W), row),
            pl.BlockSpec((tm, POOL_W), row),
            pl.BlockSpec((tm, ATTN_W), row),
            _const_spec((D_MODEL, D_MODEL)),
            _const_spec((1, D_MODEL)),
            _const_spec((1, D_MODEL)),
            _const_spec((D_MODEL, D_FF)),
            _const_spec((D_MODEL, D_FF)),
            _const_spec((D_FF, D_MODEL)),
            _const_spec((1, D_MODEL)),
            _const_spec((1, D_MODEL)),
        ],
        out_specs=pl.BlockSpec((tm, D_MODEL), row),
        out_shape=jax.ShapeDtypeStruct((T, D_MODEL), F32),
        scratch_shapes=[pltpu.VMEM((tm, D_FF), BF16)],
        compiler_params=pltpu.CompilerParams(
            dimension_semantics=("arbitrary",), vmem_limit_bytes=VMEM_LIMIT),
        name="outffn",
    )(x, yc, yp, ya, w_out, ln1_g, ln1_b, w_gate, w_up, w_down, ln2_g, ln2_b)


def _rope_tables(S):
    pos = jnp.arange(S, dtype=F32)
    inv_freq = 1.0 / (ROPE_BASE ** (jnp.arange(0, QK_ROPE_DIM, 2, dtype=F32) / QK_ROPE_DIM))
    ang = pos[:, None] * inv_freq[None, :]
    cos, sin = jnp.cos(ang), jnp.sin(ang)
    zeros = jnp.zeros((S, LANES - QK_ROPE_DIM), F32)
    cosf = jnp.concatenate([cos, cos, zeros], axis=-1)
    sinf = jnp.concatenate([-sin, sin, zeros], axis=-1)
    return cosf, sinf


def _prepare_layer(w_in, w_pw, w_pool, w_uq, w_ukv, w_out, w_gate, w_up, w_down):
    w_in_p = jnp.pad(w_in, ((0, 0), (0, D_IN_PAD - D_IN))).astype(BF16)
    wq = w_uq.reshape(Q_LORA_RANK, N_HEADS, QK_HEAD_DIM)
    wq_nope = wq[:, :, :QK_NOPE_DIM].reshape(Q_LORA_RANK, N_HEADS * QK_NOPE_DIM)
    wq_rope = jnp.pad(wq[:, :, QK_NOPE_DIM:], ((0, 0), (0, 0), (0, LANES - QK_ROPE_DIM)))
    w_uq_p = jnp.concatenate([wq_nope, wq_rope.reshape(Q_LORA_RANK, N_HEADS * LANES)], axis=-1).astype(BF16)
    wkv = w_ukv.reshape(KV_LORA_RANK, N_HEADS, QK_NOPE_DIM + V_HEAD_DIM)
    w_ukv_p = jnp.concatenate(
        [wkv[:, :, :QK_NOPE_DIM].reshape(KV_LORA_RANK, N_HEADS * QK_NOPE_DIM),
         wkv[:, :, QK_NOPE_DIM:].reshape(KV_LORA_RANK, N_HEADS * V_HEAD_DIM)], axis=-1).astype(BF16)
    w_pool_bd = jnp.zeros((POOL_W, POOL_W), F32)
    for g in range(len(POOL_WINDOWS)):
        w_pool_bd = w_pool_bd.at[g * POOL_GW:(g + 1) * POOL_GW, g * POOL_GW:(g + 1) * POOL_GW].set(w_pool[g])
    return (w_in_p, w_pw.astype(BF16), w_pool_bd.astype(BF16), w_uq_p, w_ukv_p,
            w_out.astype(BF16), w_gate.astype(BF16), w_up.astype(BF16), w_down.astype(BF16))


def _trunk(x, layers, small):
    B, S, _ = x.shape
    cosf, sinf = _rope_tables(S)
    for l in range(DEPTH):
        w_in_p, w_pw_b, w_pool_bd, w_uq_p, w_ukv_p, w_out_b, w_gate_b, w_up_b, w_down_b = layers[l]
        (w_dw, b_dw, g_cn, b_cn, pool_scale, g_q, g_kv, ln1_g, ln1_b, ln2_g, ln2_b) = small[l]
        uconv, upool, qT, k, vT = _prep_call(x, w_in_p, g_q, g_kv, w_uq_p, w_ukv_p, cosf, sinf)
        yconv, ypool = _convpool_call(uconv, upool, w_dw, b_dw, g_cn, b_cn, w_pw_b, w_pool_bd, pool_scale)
        yattn = _attn_call(qT, k, vT)
        x = _outffn_call(
            x.reshape(B * S, D_MODEL), yconv.reshape(B * S, CONV_W), ypool.reshape(B * S, POOL_W),
            yattn.reshape(B * S, ATTN_W), w_out_b, ln1_g, ln1_b, w_gate_b, w_up_b, w_down_b,
            ln2_g, ln2_b).reshape(B, S, D_MODEL)
    return x


def kernel(x_prompt, x_sample, w_in, w_dw, b_dw, g_cn, b_cn, w_pw, w_pool, pool_scale, g_q, g_kv,
           w_uq, w_ukv, w_out, ln1_g, ln1_b, w_gate, w_up, w_down, ln2_g, ln2_b):
    layers = [_prepare_layer(w_in[l], w_pw[l], w_pool[l], w_uq[l], w_ukv[l], w_out[l],
                             w_gate[l], w_up[l], w_down[l]) for l in range(DEPTH)]

    def row(a, l):
        return a[l].reshape(1, -1)

    small = [(w_dw[l], row(b_dw, l), row(g_cn, l), row(b_cn, l), row(pool_scale, l), row(g_q, l),
              row(g_kv, l), row(ln1_g, l), row(ln1_b, l), row(ln2_g, l), row(ln2_b, l))
             for l in range(DEPTH)]
    y_prompt = _trunk(x_prompt, layers, small)
    y_sample = _trunk(x_sample, layers, small)
    return (y_prompt, y_sample)
```

```python
import functools

import jax
import jax.numpy as jnp
from jax import lax
from jax.experimental import pallas as pl
from jax.experimental.pallas import tpu as pltpu

F32 = jnp.float32
BF16 = jnp.bfloat16

D_MODEL = 1024
DEPTH = 4
CONV_W = 256
CONV_KERNEL = 31
CONV_PAD = CONV_KERNEL // 2
POOL_W = 256
POOL_WINDOWS = (2, 4, 8, 16)
POOL_GW = POOL_W // len(POOL_WINDOWS)
ATTN_W = 512
N_HEADS = 4
V_HEAD_DIM = 128
QK_NOPE_DIM = 128
QK_ROPE_DIM = 64
QK_HEAD_DIM = QK_NOPE_DIM + QK_ROPE_DIM
Q_LORA_RANK = 256
KV_LORA_RANK = 128
ROPE_BASE = 10000.0
D_IN = 2 * CONV_W + POOL_W + Q_LORA_RANK + KV_LORA_RANK + QK_ROPE_DIM
D_FF = 2816
ALPHA = (2.0 * DEPTH) ** 0.25
LN_EPS = 1e-5
RMS_EPS = 1e-6
QK_SCALE_LOG2E = QK_HEAD_DIM ** -0.5 * 1.4426950408889634

LANES = 128
SUBLANES = 8
MXU_DIM = 256
D_IN_PAD = 5 * MXU_DIM
QK_PAD = 2 * LANES
HALO = 16
VMEM_LIMIT = 56 * 1024 * 1024

TM_PREP = 512
PREP_SUB = 128
CONV_CHUNK = 64
TQ = 512
TK = 512
ATTN_MAX_UNROLL = 5
TM_FFN = 512
FF_CHUNK = 256


def _const_spec(shape):
    nd = len(shape)
    return pl.BlockSpec(shape, lambda *_: (0,) * nd, pipeline_mode=pl.Buffered(1))


def _swap_rope_halves(x):
    lane = lax.broadcasted_iota(jnp.int32, x.shape, 1)
    return jnp.where(lane < QK_ROPE_DIM // 2,
                     pltpu.roll(x, LANES - QK_ROPE_DIM // 2, 1),
                     pltpu.roll(x, QK_ROPE_DIM // 2, 1))


def _prep_kernel(x_ref, w_in_ref, g_q_ref, g_kv_ref, w_uq_ref, w_ukv_ref, cos_ref, sin_ref,
                 uconv_ref, upool_ref, qT_ref, k_ref, vT_ref):
    tm = x_ref.shape[1]
    o_q = 2 * CONV_W + POOL_W
    o_kv = o_q + Q_LORA_RANK
    o_kr = o_kv + KV_LORA_RANK

    def in_proj(rows):
        xb = x_ref[0, rows, :].astype(BF16)
        return jnp.dot(xb, w_in_ref[...], preferred_element_type=F32)

    def finish(rows, u):
        uconv_ref[0, rows, :] = u[:, :2 * CONV_W]
        upool_ref[0, rows, :] = u[:, 2 * CONV_W:o_q]
        c_q = u[:, o_q:o_kv]
        c_kv = u[:, o_kv:o_kr]
        kr = u[:, o_kr:o_kr + LANES]
        ms_q = jnp.mean(c_q * c_q, axis=-1, keepdims=True)
        cqn = (c_q * lax.rsqrt(ms_q + RMS_EPS) * g_q_ref[...]).astype(BF16)
        ms_kv = jnp.mean(c_kv * c_kv, axis=-1, keepdims=True)
        ckvn = (c_kv * lax.rsqrt(ms_kv + RMS_EPS) * g_kv_ref[...]).astype(BF16)
        q = jnp.dot(cqn, w_uq_ref[...], preferred_element_type=F32)
        kv = jnp.dot(ckvn, w_ukv_ref[...], preferred_element_type=F32)
        cosf = cos_ref[rows, :]
        sinf = sin_ref[rows, :]
        kro = (kr * cosf + _swap_rope_halves(kr) * sinf).astype(BF16)
        for h in range(N_HEADS):
            qr = q[:, N_HEADS * LANES + h * LANES:N_HEADS * LANES + (h + 1) * LANES]
            qro = qr * cosf + _swap_rope_halves(qr) * sinf
            qT_ref[0, h, 0:LANES, rows] = (q[:, h * LANES:(h + 1) * LANES] * QK_SCALE_LOG2E).T.astype(BF16)
            qT_ref[0, h, LANES:QK_PAD, rows] = (qro * QK_SCALE_LOG2E).T.astype(BF16)
            k_ref[0, h, rows, 0:LANES] = kv[:, h * LANES:(h + 1) * LANES].astype(BF16)
            k_ref[0, h, rows, LANES:QK_PAD] = kro
            vT_ref[0, h, 0, :, rows] = kv[:, (N_HEADS + h) * LANES:(N_HEADS + h + 1) * LANES].T.astype(BF16)

    subs = [slice(r, r + PREP_SUB) for r in range(0, tm, PREP_SUB)]
    u_prev = in_proj(subs[0])
    for n in range(1, len(subs)):
        u_next = in_proj(subs[n])
        finish(subs[n - 1], u_prev)
        u_prev = u_next
    finish(subs[-1], u_prev)


def _prep_call(x, w_in, g_q, g_kv, w_uq, w_ukv, cosf, sinf):
    B, S, _ = x.shape
    tm = TM_PREP
    assert tm == TK
    grid = (B, S // tm)
    return pl.pallas_call(
        _prep_kernel,
        grid=grid,
        in_specs=[
            pl.BlockSpec((1, tm, D_MODEL), lambda b, i: (b, i, 0)),
            _const_spec((D_MODEL, D_IN_PAD)),
            _const_spec((1, Q_LORA_RANK)),
            _const_spec((1, KV_LORA_RANK)),
            _const_spec((Q_LORA_RANK, 2 * N_HEADS * LANES)),
            _const_spec((KV_LORA_RANK, 2 * N_HEADS * LANES)),
            pl.BlockSpec((tm, LANES), lambda b, i: (i, 0)),
            pl.BlockSpec((tm, LANES), lambda b, i: (i, 0)),
        ],
        out_specs=[
            pl.BlockSpec((1, tm, 2 * CONV_W), lambda b, i: (b, i, 0)),
            pl.BlockSpec((1, tm, POOL_W), lambda b, i: (b, i, 0)),
            pl.BlockSpec((1, N_HEADS, QK_PAD, tm), lambda b, i: (b, 0, 0, i)),
            pl.BlockSpec((1, N_HEADS, tm, QK_PAD), lambda b, i: (b, 0, i, 0)),
            pl.BlockSpec((1, N_HEADS, 1, V_HEAD_DIM, tm), lambda b, i: (b, 0, i, 0, 0)),
        ],
        out_shape=[
            jax.ShapeDtypeStruct((B, S, 2 * CONV_W), F32),
            jax.ShapeDtypeStruct((B, S, POOL_W), F32),
            jax.ShapeDtypeStruct((B, N_HEADS, QK_PAD, S), BF16),
            jax.ShapeDtypeStruct((B, N_HEADS, S, QK_PAD), BF16),
            jax.ShapeDtypeStruct((B, N_HEADS, S // tm, V_HEAD_DIM, tm), BF16),
        ],
        compiler_params=pltpu.CompilerParams(
            dimension_semantics=("arbitrary", "arbitrary"), vmem_limit_bytes=VMEM_LIMIT),
        name="prep",
    )(x, w_in, g_q, g_kv, w_uq, w_ukv, cosf, sinf)


def _attn_kernel(qT_ref, k_ref, vT_ref, qT_next_ref, k0_next_ref, o_ref,
                 sa_ref, sb_ref, cmax_ref, acc_ref, *, seq_len):
    tq = qT_ref.shape[3]
    nk = seq_len // TK
    qT = qT_ref[0, 0]

    def store_scores(k_chunk, q_t, s_ref):
        s = jnp.dot(k_chunk, q_t, preferred_element_type=F32)
        s_ref[...] = s
        return jnp.max(s, axis=0, keepdims=True)

    def scores(j, s_ref):
        start = pl.multiple_of(j * TK, TK)
        return store_scores(k_ref[0, 0, pl.ds(start, TK), :], qT, s_ref)

    def update(j, s_ref, cmax, m, l):
        m_new = jnp.maximum(m, cmax)
        alpha = jnp.exp2(m - m_new)
        p = jnp.exp2(s_ref[...] - m_new)
        l = alpha * l + jnp.sum(p, axis=0, keepdims=True)
        pv = jnp.dot(vT_ref[0, 0, j], p.astype(BF16), preferred_element_type=F32)
        acc_ref[...] = acc_ref[...] * alpha + pv
        return m_new, l

    first_step = (pl.program_id(0) == 0) & (pl.program_id(1) == 0) & (pl.program_id(2) == 0)

    @pl.when(first_step)
    def _():
        cmax_ref[...] = scores(0, sa_ref)

    acc_ref[...] = jnp.zeros_like(acc_ref)
    m = jnp.full((1, tq), -jnp.inf, F32)
    l = jnp.zeros((1, tq), F32)
    cmax_a = cmax_ref[...]

    def body(i, carry):
        m, l, cmax_a = carry
        cmax_b = scores(2 * i + 1, sb_ref)
        m, l = update(2 * i, sa_ref, cmax_a, m, l)
        cmax_a = scores(2 * i + 2, sa_ref)
        m, l = update(2 * i + 1, sb_ref, cmax_b, m, l)
        return m, l, cmax_a

    trips = nk // 2 - 1
    unroll = max(u for u in range(1, ATTN_MAX_UNROLL + 1) if trips % u == 0)
    m, l, cmax_a = lax.fori_loop(0, trips, body, (m, l, cmax_a), unroll=unroll)
    cmax_b = scores(nk - 1, sb_ref)
    m, l = update(nk - 2, sa_ref, cmax_a, m, l)
    cmax_ref[...] = store_scores(k0_next_ref[0, 0], qT_next_ref[0, 0], sa_ref)
    m, l = update(nk - 1, sb_ref, cmax_b, m, l)
    o_ref[0] = (acc_ref[...] / l).T.astype(BF16)


def _attn_call(qT, k, vT):
    B, H, S, _ = k.shape
    tq = TQ
    nq = S // tq
    assert (S // TK) % 2 == 0

    def next_step(b, h, i):
        wrap_i = i == nq - 1
        wrap_h = wrap_i & (h == H - 1)
        i2 = jnp.where(wrap_i, 0, i + 1)
        h2 = jnp.where(wrap_h, 0, jnp.where(wrap_i, h + 1, h))
        b2 = jnp.where(wrap_h, jnp.minimum(b + 1, B - 1), b)
        return b2, h2, i2

    def next_q(b, h, i):
        b2, h2, i2 = next_step(b, h, i)
        return (b2, h2, 0, i2)

    def next_k0(b, h, i):
        b2, h2, _ = next_step(b, h, i)
        return (b2, h2, 0, 0)

    return pl.pallas_call(
        functools.partial(_attn_kernel, seq_len=S),
        grid=(B, H, nq),
        in_specs=[
            pl.BlockSpec((1, 1, QK_PAD, tq), lambda b, h, i: (b, h, 0, i)),
            pl.BlockSpec((1, 1, S, QK_PAD), lambda b, h, i: (b, h, 0, 0)),
            pl.BlockSpec((1, 1, S // TK, V_HEAD_DIM, TK), lambda b, h, i: (b, h, 0, 0, 0)),
            pl.BlockSpec((1, 1, QK_PAD, tq), next_q),
            pl.BlockSpec((1, 1, TK, QK_PAD), next_k0),
        ],
        out_specs=pl.BlockSpec((1, tq, V_HEAD_DIM), lambda b, h, i: (b, i, h)),
        out_shape=jax.ShapeDtypeStruct((B, S, ATTN_W), BF16),
        scratch_shapes=[
            pltpu.VMEM((TK, tq), F32),
            pltpu.VMEM((TK, tq), F32),
            pltpu.VMEM((1, tq), F32),
            pltpu.VMEM((V_HEAD_DIM, tq), F32),
        ],
        compiler_params=pltpu.CompilerParams(
            dimension_semantics=("arbitrary", "arbitrary", "arbitrary"),
            vmem_limit_bytes=VMEM_LIMIT),
        name="attn",
    )(qT, k, vT, qT, k)


def _glu(u):
    return u[:, :CONV_W] * jax.nn.sigmoid(u[:, CONV_W:])


def _convpool_stages(uc_ref, uc_prev_ref, uc_next_ref, up_ref, up_prev_ref, up_next_ref,
                     has_prev, has_next, t0, seq_len,
                     w_dw_ref, b_dw_ref, g_cn_ref, b_cn_ref, w_pw_ref, w_pool_ref, pscale_ref,
                     hs_ref, hsh_ref, ps_ref, hn_ref, out_ref):
    tm = uc_ref.shape[0]
    shift_rows = hsh_ref.shape[1]

    def halo_stage():
        if uc_prev_ref is None:
            hs_ref[0:HALO, :] = jnp.zeros((HALO, CONV_W), F32)
            ps_ref[0:HALO, :] = jnp.zeros((HALO, POOL_W), F32)
        else:
            hs_ref[0:HALO, :] = _glu(uc_prev_ref[...]) * has_prev
            ps_ref[0:HALO, :] = up_prev_ref[...] * has_prev
        hs_ref[HALO:HALO + tm, :] = _glu(uc_ref[...])
        hs_ref[HALO + tm:2 * HALO + tm, :] = _glu(uc_next_ref[...]) * has_next
        ps_ref[HALO:HALO + tm, :] = up_ref[...]
        ps_ref[HALO + tm:2 * HALO + tm, :] = up_next_ref[...] * has_next

    def shift_stage():
        for s in range(1, SUBLANES):
            hsh_ref[s - 1, 0:shift_rows, :] = hs_ref[s:s + shift_rows, :]

    def tap(c, k):
        off = HALO - CONV_PAD + k
        s, row = off % SUBLANES, c * CONV_CHUNK + off - off % SUBLANES
        if s == 0:
            return hs_ref[row:row + CONV_CHUNK, :]
        return hsh_ref[s - 1, row:row + CONV_CHUNK, :]

    def conv_stage(c):
        def run():
            acc = jnp.broadcast_to(b_dw_ref[...], (CONV_CHUNK, CONV_W))
            for k in range(CONV_KERNEL):
                acc = acc + tap(c, k) * w_dw_ref[k:k + 1, :]
            mu = jnp.mean(acc, axis=-1, keepdims=True)
            xc = acc - mu
            var = jnp.mean(xc * xc, axis=-1, keepdims=True)
            hn = xc * lax.rsqrt(var + LN_EPS) * g_cn_ref[...] + b_cn_ref[...]
            hn_ref[c * CONV_CHUNK:(c + 1) * CONV_CHUNK, :] = (hn * jax.nn.sigmoid(hn)).astype(BF16)
        return run

    def pointwise_stage():
        out_ref[:, 0:CONV_W] = jnp.dot(
            hn_ref[...], w_pw_ref[...], preferred_element_type=F32).astype(BF16)

    def pool_stage():
        def shifted(d, lo, hi):
            return ps_ref[HALO + d:HALO + d + tm, lo:hi]

        t = t0 + lax.broadcasted_iota(jnp.int32, (tm, 1), 0)

        def count(w):
            left = w // 2
            right = w - 1 - left
            return (jnp.minimum(t + right, seq_len - 1) - jnp.maximum(t - left, 0) + 1).astype(F32)

        lane = lax.broadcasted_iota(jnp.int32, (tm, LANES), 1)
        first = lane < POOL_GW
        s2 = shifted(-1, 0, LANES) + shifted(0, 0, LANES)
        s4 = s2 + shifted(-2, 0, LANES) + shifted(1, 0, LANES)
        mean_lo = jnp.where(first, s2 / count(2), s4 / count(4))
        s8 = shifted(-4, LANES, 2 * LANES)
        for d in range(-3, 4):
            s8 = s8 + shifted(d, LANES, 2 * LANES)
        s16 = s8
        for d in (-8, -7, -6, -5, 4, 5, 6, 7):
            s16 = s16 + shifted(d, LANES, 2 * LANES)
        mean_hi = jnp.where(first, s8 / count(8), s16 / count(16))
        d_lo = (mean_lo - shifted(0, 0, LANES)).astype(BF16)
        d_hi = (mean_hi - shifted(0, LANES, 2 * LANES)).astype(BF16)
        d = jnp.concatenate([d_lo, d_hi], axis=-1)
        y = jnp.dot(d, w_pool_ref[...], preferred_element_type=F32) * pscale_ref[...]
        out_ref[:, CONV_W:CONV_W + POOL_W] = y.astype(BF16)

    def mix_stage():
        pointwise_stage()
        pool_stage()

    return [halo_stage, shift_stage] + [conv_stage(c) for c in range(tm // CONV_CHUNK)] + [mix_stage]


def _layer_norm(x, g, b):
    mu = jnp.mean(x, axis=-1, keepdims=True)
    xc = x - mu
    var = jnp.mean(xc * xc, axis=-1, keepdims=True)
    return xc * lax.rsqrt(var + LN_EPS) * g + b


def _mixffn_kernel(x_ref, ya_ref,
                   uc0_ref, uc0_next_ref, up0_ref, up0_next_ref,
                   uc_ref, uc_prev_ref, uc_next_ref, up_ref, up_prev_ref, up_next_ref,
                   w_dw_ref, b_dw_ref, g_cn_ref, b_cn_ref, w_pw_ref, w_pool_ref, pscale_ref,
                   w_out_ref, ln1_g_ref, ln1_b_ref, w_gate_ref, w_up_ref, w_down_ref,
                   ln2_g_ref, ln2_b_ref,
                   o_ref, ycp_ref, hs_ref, hsh_ref, ps_ref, hn_ref, h_ref, *, seq_len):
    tm = x_ref.shape[0]
    tiles_per_seq = seq_len // tm
    i = pl.program_id(0)
    slot = i % 2
    conv_params = (w_dw_ref, b_dw_ref, g_cn_ref, b_cn_ref, w_pw_ref, w_pool_ref, pscale_ref,
                   hs_ref, hsh_ref, ps_ref, hn_ref)

    @pl.when(i == 0)
    def _():
        for stage in _convpool_stages(uc0_ref, None, uc0_next_ref, up0_ref, None, up0_next_ref,
                                      0.0, 1.0 if tiles_per_seq > 1 else 0.0, 0, seq_len,
                                      *conv_params, ycp_ref.at[0]):
            stage()

    j = jnp.minimum(i + 1, pl.num_programs(0) - 1) % tiles_per_seq
    next_stages = _convpool_stages(
        uc_ref, uc_prev_ref, uc_next_ref, up_ref, up_prev_ref, up_next_ref,
        (j > 0).astype(F32), (j < tiles_per_seq - 1).astype(F32), j * tm, seq_len,
        *conv_params, ycp_ref.at[1 - slot])
    n_chunks = D_FF // FF_CHUNK
    assert len(next_stages) <= n_chunks

    ycp = ycp_ref[slot]
    mix = jnp.dot(ycp, w_out_ref[0:CONV_W + POOL_W, :], preferred_element_type=F32)
    mix += jnp.dot(ya_ref[...], w_out_ref[CONV_W + POOL_W:, :], preferred_element_type=F32)
    x1 = _layer_norm(ALPHA * x_ref[...] + mix, ln1_g_ref[...], ln1_b_ref[...])
    x1b = x1.astype(BF16)
    for c in range(n_chunks):
        cols = slice(c * FF_CHUNK, (c + 1) * FF_CHUNK)
        g = jnp.dot(x1b, w_gate_ref[:, cols], preferred_element_type=F32)
        u = jnp.dot(x1b, w_up_ref[:, cols], preferred_element_type=F32)
        h_ref[:, cols] = (g * jax.nn.sigmoid(g) * u).astype(BF16)
        if c < len(next_stages):
            next_stages[c]()
    y = jnp.dot(h_ref[...], w_down_ref[...], preferred_element_type=F32)
    o_ref[...] = _layer_norm(ALPHA * x1 + y, ln2_g_ref[...], ln2_b_ref[...])


def _mixffn_call(x, ya, uconv, upool, seq_len, w_dw, b_dw, g_cn, b_cn, w_pw, w_pool_bd, pool_scale,
                 w_out, ln1_g, ln1_b, w_gate, w_up, w_down, ln2_g, ln2_b):
    T = x.shape[0]
    tm = TM_FFN
    n = T // tm
    r = tm // HALO
    n_halo_blocks = T // HALO

    def row(i):
        return (i, 0)

    def nxt_tile(i):
        return (jnp.minimum(i + 1, n - 1), 0)

    def nxt_tile_prev_halo(i):
        return (jnp.maximum(jnp.minimum(i + 1, n - 1) * r - 1, 0), 0)

    def nxt_tile_next_halo(i):
        return (jnp.minimum((jnp.minimum(i + 1, n - 1) + 1) * r, n_halo_blocks - 1), 0)

    def first_tile(i):
        return (0, 0)

    def first_tile_next_halo(i):
        return (r, 0)

    return pl.pallas_call(
        functools.partial(_mixffn_kernel, seq_len=seq_len),
        grid=(n,),
        in_specs=[
            pl.BlockSpec((tm, D_MODEL), row),
            pl.BlockSpec((tm, ATTN_W), row),
            pl.BlockSpec((tm, 2 * CONV_W), first_tile),
            pl.BlockSpec((HALO, 2 * CONV_W), first_tile_next_halo),
            pl.BlockSpec((tm, POOL_W), first_tile),
            pl.BlockSpec((HALO, POOL_W), first_tile_next_halo),
            pl.BlockSpec((tm, 2 * CONV_W), nxt_tile),
            pl.BlockSpec((HALO, 2 * CONV_W), nxt_tile_prev_halo),
            pl.BlockSpec((HALO, 2 * CONV_W), nxt_tile_next_halo),
            pl.BlockSpec((tm, POOL_W), nxt_tile),
            pl.BlockSpec((HALO, POOL_W), nxt_tile_prev_halo),
            pl.BlockSpec((HALO, POOL_W), nxt_tile_next_halo),
            _const_spec((CONV_KERNEL, CONV_W)),
            _const_spec((1, CONV_W)),
            _const_spec((1, CONV_W)),
            _const_spec((1, CONV_W)),
            _const_spec((CONV_W, CONV_W)),
            _const_spec((POOL_W, POOL_W)),
            _const_spec((1, POOL_W)),
            _const_spec((D_MODEL, D_MODEL)),
            _const_spec((1, D_MODEL)),
            _const_spec((1, D_MODEL)),
            _const_spec((D_MODEL, D_FF)),
            _const_spec((D_MODEL, D_FF)),
            _const_spec((D_FF, D_MODEL)),
            _const_spec((1, D_MODEL)),
            _const_spec((1, D_MODEL)),
        ],
        out_specs=pl.BlockSpec((tm, D_MODEL), row),
        out_shape=jax.ShapeDtypeStruct((T, D_MODEL), F32),
        scratch_shapes=[
            pltpu.VMEM((2, tm, CONV_W + POOL_W), BF16),
            pltpu.VMEM((tm + 2 * HALO, CONV_W), F32),
            pltpu.VMEM((SUBLANES - 1, tm + 2 * HALO - SUBLANES, CONV_W), F32),
            pltpu.VMEM((tm + 2 * HALO, POOL_W), F32),
            pltpu.VMEM((tm, CONV_W), BF16),
            pltpu.VMEM((tm, D_FF), BF16),
        ],
        compiler_params=pltpu.CompilerParams(
            dimension_semantics=("arbitrary",), vmem_limit_bytes=VMEM_LIMIT),
        name="mixffn",
    )(x, ya, uconv, uconv, upool, upool, uconv, uconv, uconv, upool, upool, upool,
      w_dw, b_dw, g_cn, b_cn, w_pw, w_pool_bd, pool_scale,
      w_out, ln1_g, ln1_b, w_gate, w_up, w_down, ln2_g, ln2_b)


def _rope_tables(S):
    pos = jnp.arange(S, dtype=F32)
    inv_freq = 1.0 / (ROPE_BASE ** (jnp.arange(0, QK_ROPE_DIM, 2, dtype=F32) / QK_ROPE_DIM))
    ang = pos[:, None] * inv_freq[None, :]
    cos, sin = jnp.cos(ang), jnp.sin(ang)
    zeros = jnp.zeros((S, LANES - QK_ROPE_DIM), F32)
    cosf = jnp.concatenate([cos, cos, zeros], axis=-1)
    sinf = jnp.concatenate([-sin, sin, zeros], axis=-1)
    return cosf, sinf


def _prepare_layer(w_in, w_pw, w_pool, w_uq, w_ukv, w_out, w_gate, w_up, w_down):
    w_in_p = jnp.pad(w_in, ((0, 0), (0, D_IN_PAD - D_IN))).astype(BF16)
    wq = w_uq.reshape(Q_LORA_RANK, N_HEADS, QK_HEAD_DIM)
    wq_nope = wq[:, :, :QK_NOPE_DIM].reshape(Q_LORA_RANK, N_HEADS * QK_NOPE_DIM)
    wq_rope = jnp.pad(wq[:, :, QK_NOPE_DIM:], ((0, 0), (0, 0), (0, LANES - QK_ROPE_DIM)))
    w_uq_p = jnp.concatenate([wq_nope, wq_rope.reshape(Q_LORA_RANK, N_HEADS * LANES)], axis=-1).astype(BF16)
    wkv = w_ukv.reshape(KV_LORA_RANK, N_HEADS, QK_NOPE_DIM + V_HEAD_DIM)
    w_ukv_p = jnp.concatenate(
        [wkv[:, :, :QK_NOPE_DIM].reshape(KV_LORA_RANK, N_HEADS * QK_NOPE_DIM),
         wkv[:, :, QK_NOPE_DIM:].reshape(KV_LORA_RANK, N_HEADS * V_HEAD_DIM)], axis=-1).astype(BF16)
    w_pool_bd = jnp.zeros((POOL_W, POOL_W), F32)
    for g in range(len(POOL_WINDOWS)):
        w_pool_bd = w_pool_bd.at[g * POOL_GW:(g + 1) * POOL_GW, g * POOL_GW:(g + 1) * POOL_GW].set(w_pool[g])
    return (w_in_p, w_pw.astype(BF16), w_pool_bd.astype(BF16), w_uq_p, w_ukv_p,
            w_out.astype(BF16), w_gate.astype(BF16), w_up.astype(BF16), w_down.astype(BF16))


def _trunk(x, layers, small):
    B, S, _ = x.shape
    cosf, sinf = _rope_tables(S)
    for l in range(DEPTH):
        w_in_p, w_pw_b, w_pool_bd, w_uq_p, w_ukv_p, w_out_b, w_gate_b, w_up_b, w_down_b = layers[l]
        (w_dw, b_dw, g_cn, b_cn, pool_scale, g_q, g_kv, ln1_g, ln1_b, ln2_g, ln2_b) = small[l]
        uconv, upool, qT, k, vT = _prep_call(x, w_in_p, g_q, g_kv, w_uq_p, w_ukv_p, cosf, sinf)
        yattn = _attn_call(qT, k, vT)
        x = _mixffn_call(
            x.reshape(B * S, D_MODEL), yattn.reshape(B * S, ATTN_W),
            uconv.reshape(B * S, 2 * CONV_W), upool.reshape(B * S, POOL_W), S,
            w_dw, b_dw, g_cn, b_cn, w_pw_b, w_pool_bd, pool_scale,
            w_out_b, ln1_g, ln1_b, w_gate_b, w_up_b, w_down_b, ln2_g, ln2_b).reshape(B, S, D_MODEL)
    return x


def kernel(x_prompt, x_sample, w_in, w_dw, b_dw, g_cn, b_cn, w_pw, w_pool, pool_scale, g_q, g_kv,
           w_uq, w_ukv, w_out, ln1_g, ln1_b, w_gate, w_up, w_down, ln2_g, ln2_b):
    layers = [_prepare_layer(w_in[l], w_pw[l], w_pool[l], w_uq[l], w_ukv[l], w_out[l],
                             w_gate[l], w_up[l], w_down[l]) for l in range(DEPTH)]

    def row(a, l):
        return a[l].reshape(1, -1)

    small = [(w_dw[l], row(b_dw, l), row(g_cn, l), row(b_cn, l), row(pool_scale, l), row(g_q, l),
              row(g_kv, l), row(ln1_g, l), row(ln1_b, l), row(ln2_g, l), row(ln2_b, l))
             for l in range(DEPTH)]
    y_prompt = _trunk(x_prompt, layers, small)
    y_sample = _trunk(x_sample, layers, small)
    return (y_prompt, y_sample)
```

```python
import functools

import jax
import jax.numpy as jnp
from jax import lax
from jax.experimental import pallas as pl
from jax.experimental.pallas import tpu as pltpu

F32 = jnp.float32
BF16 = jnp.bfloat16

D_MODEL = 1024
DEPTH = 4
CONV_W = 256
CONV_KERNEL = 31
CONV_PAD = CONV_KERNEL // 2
POOL_W = 256
POOL_WINDOWS = (2, 4, 8, 16)
POOL_GW = POOL_W // len(POOL_WINDOWS)
ATTN_W = 512
N_HEADS = 4
V_HEAD_DIM = 128
QK_NOPE_DIM = 128
QK_ROPE_DIM = 64
QK_HEAD_DIM = QK_NOPE_DIM + QK_ROPE_DIM
Q_LORA_RANK = 256
KV_LORA_RANK = 128
ROPE_BASE = 10000.0
D_IN = 2 * CONV_W + POOL_W + Q_LORA_RANK + KV_LORA_RANK + QK_ROPE_DIM
D_FF = 2816
ALPHA = (2.0 * DEPTH) ** 0.25
LN_EPS = 1e-5
RMS_EPS = 1e-6
QK_SCALE_LOG2E = QK_HEAD_DIM ** -0.5 * 1.4426950408889634

LANES = 128
SUBLANES = 8
MXU_DIM = 256
D_IN_PAD = 5 * MXU_DIM
QK_PAD = 2 * LANES
V_ROWS = V_HEAD_DIM + 16
HALO = 16
VMEM_LIMIT = 56 * 1024 * 1024

TM_PREP = 512
PREP_SUB = 128
CONV_CHUNK = 64
TQ = 512
TK = 512
ATTN_MAX_UNROLL = 5
TM_FFN = 512
FF_CHUNK = 256


def _const_spec(shape):
    nd = len(shape)
    return pl.BlockSpec(shape, lambda *_: (0,) * nd, pipeline_mode=pl.Buffered(1))


def _swap_rope_halves(x):
    lane = lax.broadcasted_iota(jnp.int32, x.shape, 1)
    return jnp.where(lane < QK_ROPE_DIM // 2,
                     pltpu.roll(x, LANES - QK_ROPE_DIM // 2, 1),
                     pltpu.roll(x, QK_ROPE_DIM // 2, 1))


def _prep_kernel(x_ref, w_in_ref, g_q_ref, g_kv_ref, w_uq_ref, w_ukv_ref, cos_ref, sin_ref,
                 uconv_ref, upool_ref, qT_ref, k_ref, vT_ref):
    tm = x_ref.shape[1]
    o_q = 2 * CONV_W + POOL_W
    o_kv = o_q + Q_LORA_RANK
    o_kr = o_kv + KV_LORA_RANK

    def in_proj(rows):
        xb = x_ref[0, rows, :].astype(BF16)
        return jnp.dot(xb, w_in_ref[...], preferred_element_type=F32)

    def finish(rows, u):
        uconv_ref[0, rows, :] = u[:, :2 * CONV_W]
        upool_ref[0, rows, :] = u[:, 2 * CONV_W:o_q]
        c_q = u[:, o_q:o_kv]
        c_kv = u[:, o_kv:o_kr]
        kr = u[:, o_kr:o_kr + LANES]
        ms_q = jnp.mean(c_q * c_q, axis=-1, keepdims=True)
        cqn = (c_q * lax.rsqrt(ms_q + RMS_EPS) * g_q_ref[...]).astype(BF16)
        ms_kv = jnp.mean(c_kv * c_kv, axis=-1, keepdims=True)
        ckvn = (c_kv * lax.rsqrt(ms_kv + RMS_EPS) * g_kv_ref[...]).astype(BF16)
        q = jnp.dot(cqn, w_uq_ref[...], preferred_element_type=F32)
        kv = jnp.dot(ckvn, w_ukv_ref[...], preferred_element_type=F32)
        cosf = cos_ref[rows, :]
        sinf = sin_ref[rows, :]
        kro = (kr * cosf + _swap_rope_halves(kr) * sinf).astype(BF16)
        for h in range(N_HEADS):
            qr = q[:, N_HEADS * LANES + h * LANES:N_HEADS * LANES + (h + 1) * LANES]
            qro = qr * cosf + _swap_rope_halves(qr) * sinf
            qT_ref[0, h, 0:LANES, rows] = (q[:, h * LANES:(h + 1) * LANES] * QK_SCALE_LOG2E).T.astype(BF16)
            qT_ref[0, h, LANES:QK_PAD, rows] = (qro * QK_SCALE_LOG2E).T.astype(BF16)
            k_ref[0, h, rows, 0:LANES] = kv[:, h * LANES:(h + 1) * LANES].astype(BF16)
            k_ref[0, h, rows, LANES:QK_PAD] = kro
            vT_ref[0, h, 0, 0:V_HEAD_DIM, rows] = (
                kv[:, (N_HEADS + h) * LANES:(N_HEADS + h + 1) * LANES].T.astype(BF16))
            vT_ref[0, h, 0, V_HEAD_DIM:V_ROWS, rows] = jnp.ones((V_ROWS - V_HEAD_DIM, PREP_SUB), BF16)

    subs = [slice(r, r + PREP_SUB) for r in range(0, tm, PREP_SUB)]
    u_prev = in_proj(subs[0])
    for n in range(1, len(subs)):
        u_next = in_proj(subs[n])
        finish(subs[n - 1], u_prev)
        u_prev = u_next
    finish(subs[-1], u_prev)


def _prep_call(x, w_in, g_q, g_kv, w_uq, w_ukv, cosf, sinf):
    B, S, _ = x.shape
    tm = TM_PREP
    assert tm == TK
    grid = (B, S // tm)
    return pl.pallas_call(
        _prep_kernel,
        grid=grid,
        in_specs=[
            pl.BlockSpec((1, tm, D_MODEL), lambda b, i: (b, i, 0)),
            _const_spec((D_MODEL, D_IN_PAD)),
            _const_spec((1, Q_LORA_RANK)),
            _const_spec((1, KV_LORA_RANK)),
            _const_spec((Q_LORA_RANK, 2 * N_HEADS * LANES)),
            _const_spec((KV_LORA_RANK, 2 * N_HEADS * LANES)),
            pl.BlockSpec((tm, LANES), lambda b, i: (i, 0)),
            pl.BlockSpec((tm, LANES), lambda b, i: (i, 0)),
        ],
        out_specs=[
            pl.BlockSpec((1, tm, 2 * CONV_W), lambda b, i: (b, i, 0)),
            pl.BlockSpec((1, tm, POOL_W), lambda b, i: (b, i, 0)),
            pl.BlockSpec((1, N_HEADS, QK_PAD, tm), lambda b, i: (b, 0, 0, i)),
            pl.BlockSpec((1, N_HEADS, tm, QK_PAD), lambda b, i: (b, 0, i, 0)),
            pl.BlockSpec((1, N_HEADS, 1, V_ROWS, tm), lambda b, i: (b, 0, i, 0, 0)),
        ],
        out_shape=[
            jax.ShapeDtypeStruct((B, S, 2 * CONV_W), F32),
            jax.ShapeDtypeStruct((B, S, POOL_W), F32),
            jax.ShapeDtypeStruct((B, N_HEADS, QK_PAD, S), BF16),
            jax.ShapeDtypeStruct((B, N_HEADS, S, QK_PAD), BF16),
            jax.ShapeDtypeStruct((B, N_HEADS, S // tm, V_ROWS, tm), BF16),
        ],
        compiler_params=pltpu.CompilerParams(
            dimension_semantics=("arbitrary", "arbitrary"), vmem_limit_bytes=VMEM_LIMIT),
        name="prep",
    )(x, w_in, g_q, g_kv, w_uq, w_ukv, cosf, sinf)


def _attn_kernel(qT_ref, k_ref, vT_ref, qT_next_ref, k0_next_ref, o_ref,
                 sa_ref, sb_ref, cmax_ref, acc_ref, *, seq_len, prefetch_next):
    tq = qT_ref.shape[3]
    nk = seq_len // TK
    qT = qT_ref[0, 0]

    def store_scores(k_chunk, q_t, s_ref):
        s = jnp.dot(k_chunk, q_t, preferred_element_type=F32)
        s_ref[...] = s
        return jnp.max(s, axis=0, keepdims=True)

    def scores(j, s_ref):
        start = pl.multiple_of(j * TK, TK)
        return store_scores(k_ref[0, 0, pl.ds(start, TK), :], qT, s_ref)

    def update(j, s_ref, cmax, m):
        m_new = jnp.maximum(m, cmax)
        alpha = jnp.exp2(m - m_new)
        p = jnp.exp2(s_ref[...] - m_new)
        pv = jnp.dot(vT_ref[0, 0, j], p.astype(BF16), preferred_element_type=F32)
        acc_ref[...] = acc_ref[...] * alpha + pv
        return m_new

    if prefetch_next:
        first_step = (pl.program_id(0) == 0) & (pl.program_id(1) == 0) & (pl.program_id(2) == 0)

        @pl.when(first_step)
        def _():
            cmax_ref[...] = scores(0, sa_ref)

        cmax_a = cmax_ref[...]
    else:
        cmax_a = scores(0, sa_ref)
    acc_ref[...] = jnp.zeros_like(acc_ref)
    m = jnp.full((1, tq), -jnp.inf, F32)

    def body(i, carry):
        m, cmax_a = carry
        cmax_b = scores(2 * i + 1, sb_ref)
        m = update(2 * i, sa_ref, cmax_a, m)
        cmax_a = scores(2 * i + 2, sa_ref)
        m = update(2 * i + 1, sb_ref, cmax_b, m)
        return m, cmax_a

    trips = nk // 2 - 1
    unroll = max(u for u in range(1, ATTN_MAX_UNROLL + 1) if trips % u == 0)
    m, cmax_a = lax.fori_loop(0, trips, body, (m, cmax_a), unroll=unroll)
    cmax_b = scores(nk - 1, sb_ref)
    m = update(nk - 2, sa_ref, cmax_a, m)
    if prefetch_next:
        cmax_ref[...] = store_scores(k0_next_ref[0, 0], qT_next_ref[0, 0], sa_ref)
    m = update(nk - 1, sb_ref, cmax_b, m)
    l = acc_ref[V_HEAD_DIM:V_HEAD_DIM + 1, :]
    o_ref[0] = (acc_ref[0:V_HEAD_DIM, :] / l).T.astype(BF16)


def _attn_call(qT, k, vT):
    B, H, S, _ = k.shape
    tq = TQ
    nq = S // tq
    assert (S // TK) % 2 == 0

    def next_step(b, h, i):
        wrap_i = i == nq - 1
        wrap_h = wrap_i & (h == H - 1)
        i2 = jnp.where(wrap_i, 0, i + 1)
        h2 = jnp.where(wrap_h, 0, jnp.where(wrap_i, h + 1, h))
        b2 = jnp.where(wrap_h, jnp.minimum(b + 1, B - 1), b)
        return b2, h2, i2

    prefetch_next = (S // TK) // 2 - 1 <= ATTN_MAX_UNROLL

    def next_q(b, h, i):
        if not prefetch_next:
            return (0, 0, 0, 0)
        b2, h2, i2 = next_step(b, h, i)
        return (b2, h2, 0, i2)

    def next_k0(b, h, i):
        if not prefetch_next:
            return (0, 0, 0, 0)
        b2, h2, _ = next_step(b, h, i)
        return (b2, h2, 0, 0)

    return pl.pallas_call(
        functools.partial(_attn_kernel, seq_len=S, prefetch_next=prefetch_next),
        grid=(B, H, nq),
        in_specs=[
            pl.BlockSpec((1, 1, QK_PAD, tq), lambda b, h, i: (b, h, 0, i)),
            pl.BlockSpec((1, 1, S, QK_PAD), lambda b, h, i: (b, h, 0, 0)),
            pl.BlockSpec((1, 1, S // TK, V_ROWS, TK), lambda b, h, i: (b, h, 0, 0, 0)),
            pl.BlockSpec((1, 1, QK_PAD, tq), next_q),
            pl.BlockSpec((1, 1, TK, QK_PAD), next_k0),
        ],
        out_specs=pl.BlockSpec((1, tq, V_HEAD_DIM), lambda b, h, i: (b, i, h)),
        out_shape=jax.ShapeDtypeStruct((B, S, ATTN_W), BF16),
        scratch_shapes=[
            pltpu.VMEM((TK, tq), F32),
            pltpu.VMEM((TK, tq), F32),
            pltpu.VMEM((1, tq), F32),
            pltpu.VMEM((V_ROWS, tq), F32),
        ],
        compiler_params=pltpu.CompilerParams(
            dimension_semantics=("arbitrary", "arbitrary", "arbitrary"),
            vmem_limit_bytes=VMEM_LIMIT),
        name="attn",
    )(qT, k, vT, qT, k)


def _glu(u):
    return u[:, :CONV_W] * jax.nn.sigmoid(u[:, CONV_W:])


def _convpool_stages(uc_ref, uc_prev_ref, uc_next_ref, up_ref, up_prev_ref, up_next_ref,
                     has_prev, has_next, cnt_ref,
                     w_dw_ref, b_dw_ref, g_cn_ref, b_cn_ref, w_pw_ref, w_pool_ref, pscale_ref,
                     hs_ref, hsh_ref, ps_ref, hn_ref, out_ref):
    tm = uc_ref.shape[0]
    shift_rows = hsh_ref.shape[1]

    def halo_stage():
        if uc_prev_ref is None:
            hs_ref[0:HALO, :] = jnp.zeros((HALO, CONV_W), F32)
            ps_ref[0:HALO, :] = jnp.zeros((HALO, POOL_W), F32)
        else:
            hs_ref[0:HALO, :] = _glu(uc_prev_ref[...]) * has_prev
            ps_ref[0:HALO, :] = up_prev_ref[...] * has_prev
        hs_ref[HALO:HALO + tm, :] = _glu(uc_ref[...])
        hs_ref[HALO + tm:2 * HALO + tm, :] = _glu(uc_next_ref[...]) * has_next
        ps_ref[HALO:HALO + tm, :] = up_ref[...]
        ps_ref[HALO + tm:2 * HALO + tm, :] = up_next_ref[...] * has_next

    def shift_stage():
        for s in range(1, SUBLANES):
            hsh_ref[s - 1, 0:shift_rows, :] = hs_ref[s:s + shift_rows, :]

    def tap(c, k):
        off = HALO - CONV_PAD + k
        s, row = off % SUBLANES, c * CONV_CHUNK + off - off % SUBLANES
        if s == 0:
            return hs_ref[row:row + CONV_CHUNK, :]
        return hsh_ref[s - 1, row:row + CONV_CHUNK, :]

    def conv_stage(c):
        def run():
            groups = (CONV_CHUNK // SUBLANES, SUBLANES, CONV_W)
            acc = jnp.broadcast_to(b_dw_ref[...], groups)
            for k in range(CONV_KERNEL):
                acc = acc + tap(c, k).reshape(groups) * w_dw_ref[k]
            acc = acc.reshape(CONV_CHUNK, CONV_W)
            mu = jnp.mean(acc, axis=-1, keepdims=True)
            xc = acc - mu
            var = jnp.mean(xc * xc, axis=-1, keepdims=True)
            hn = xc * lax.rsqrt(var + LN_EPS) * g_cn_ref[...] + b_cn_ref[...]
            hn_ref[c * CONV_CHUNK:(c + 1) * CONV_CHUNK, :] = (hn * jax.nn.sigmoid(hn)).astype(BF16)
        return run

    def pointwise_stage():
        out_ref[:, 0:CONV_W] = jnp.dot(
            hn_ref[...], w_pw_ref[...], preferred_element_type=F32).astype(BF16)

    def pool_stage():
        def shifted(d, lo, hi):
            return ps_ref[HALO + d:HALO + d + tm, lo:hi]

        lane = lax.broadcasted_iota(jnp.int32, (tm, LANES), 1)
        first = lane < POOL_GW
        s2 = shifted(-1, 0, LANES) + shifted(0, 0, LANES)
        s4 = s2 + shifted(-2, 0, LANES) + shifted(1, 0, LANES)
        mean_lo = jnp.where(first, s2, s4) / cnt_ref[:, 0:LANES]
        s8 = shifted(-4, LANES, 2 * LANES)
        for d in range(-3, 4):
            s8 = s8 + shifted(d, LANES, 2 * LANES)
        s16 = s8
        for d in (-8, -7, -6, -5, 4, 5, 6, 7):
            s16 = s16 + shifted(d, LANES, 2 * LANES)
        mean_hi = jnp.where(first, s8, s16) / cnt_ref[:, LANES:2 * LANES]
        d_lo = (mean_lo - shifted(0, 0, LANES)).astype(BF16)
        d_hi = (mean_hi - shifted(0, LANES, 2 * LANES)).astype(BF16)
        d = jnp.concatenate([d_lo, d_hi], axis=-1)
        y = jnp.dot(d, w_pool_ref[...], preferred_element_type=F32) * pscale_ref[...]
        out_ref[:, CONV_W:CONV_W + POOL_W] = y.astype(BF16)

    def mix_stage():
        pointwise_stage()
        pool_stage()

    return [halo_stage, shift_stage] + [conv_stage(c) for c in range(tm // CONV_CHUNK)] + [mix_stage]


def _layer_norm(x, g, b):
    mu = jnp.mean(x, axis=-1, keepdims=True)
    xc = x - mu
    var = jnp.mean(xc * xc, axis=-1, keepdims=True)
    return xc * lax.rsqrt(var + LN_EPS) * g + b


def _mixffn_kernel(x_ref, ya_ref,
                   uc0_ref, uc0_next_ref, up0_ref, up0_next_ref, cnt0_ref,
                   uc_ref, uc_prev_ref, uc_next_ref, up_ref, up_prev_ref, up_next_ref, cnt_ref,
                   w_dw_ref, b_dw_ref, g_cn_ref, b_cn_ref, w_pw_ref, w_pool_ref, pscale_ref,
                   w_out_ref, ln1_g_ref, ln1_b_ref, w_gate_ref, w_up_ref, w_down_ref,
                   ln2_g_ref, ln2_b_ref,
                   o_ref, ycp_ref, hs_ref, hsh_ref, ps_ref, hn_ref, h_ref, *, seq_len):
    tm = x_ref.shape[0]
    tiles_per_seq = seq_len // tm
    i = pl.program_id(0)
    slot = i % 2
    conv_params = (w_dw_ref, b_dw_ref, g_cn_ref, b_cn_ref, w_pw_ref, w_pool_ref, pscale_ref,
                   hs_ref, hsh_ref, ps_ref, hn_ref)

    @pl.when(i == 0)
    def _():
        for stage in _convpool_stages(uc0_ref, None, uc0_next_ref, up0_ref, None, up0_next_ref,
                                      0.0, 1.0 if tiles_per_seq > 1 else 0.0, cnt0_ref,
                                      *conv_params, ycp_ref.at[0]):
            stage()

    j = jnp.minimum(i + 1, pl.num_programs(0) - 1) % tiles_per_seq
    next_stages = _convpool_stages(
        uc_ref, uc_prev_ref, uc_next_ref, up_ref, up_prev_ref, up_next_ref,
        (j > 0).astype(F32), (j < tiles_per_seq - 1).astype(F32), cnt_ref,
        *conv_params, ycp_ref.at[1 - slot])
    n_chunks = D_FF // FF_CHUNK
    assert len(next_stages) <= n_chunks

    ycp = ycp_ref[slot]
    mix = jnp.dot(ycp, w_out_ref[0:CONV_W + POOL_W, :], preferred_element_type=F32)
    mix += jnp.dot(ya_ref[...], w_out_ref[CONV_W + POOL_W:, :], preferred_element_type=F32)
    x1 = _layer_norm(ALPHA * x_ref[...] + mix, ln1_g_ref[...], ln1_b_ref[...])
    x1b = x1.astype(BF16)
    for c in range(n_chunks):
        cols = slice(c * FF_CHUNK, (c + 1) * FF_CHUNK)
        g = jnp.dot(x1b, w_gate_ref[:, cols], preferred_element_type=F32)
        u = jnp.dot(x1b, w_up_ref[:, cols], preferred_element_type=F32)
        if c < len(next_stages):
            next_stages[c]()
        h_ref[:, cols] = (g * jax.nn.sigmoid(g) * u).astype(BF16)
    y = jnp.dot(h_ref[...], w_down_ref[...], preferred_element_type=F32)
    o_ref[...] = _layer_norm(ALPHA * x1 + y, ln2_g_ref[...], ln2_b_ref[...])


def _mixffn_call(x, ya, uconv, upool, pool_cnt, seq_len, w_dw, b_dw, g_cn, b_cn, w_pw, w_pool_bd, pool_scale,
                 w_out, ln1_g, ln1_b, w_gate, w_up, w_down, ln2_g, ln2_b):
    T = x.shape[0]
    tm = TM_FFN
    n = T // tm
    r = tm // HALO
    n_halo_blocks = T // HALO

    def row(i):
        return (i, 0)

    def nxt_tile(i):
        return (jnp.minimum(i + 1, n - 1), 0)

    def nxt_tile_prev_halo(i):
        return (jnp.maximum(jnp.minimum(i + 1, n - 1) * r - 1, 0), 0)

    def nxt_tile_next_halo(i):
        return (jnp.minimum((jnp.minimum(i + 1, n - 1) + 1) * r, n_halo_blocks - 1), 0)

    def first_tile(i):
        return (0, 0)

    def first_tile_next_halo(i):
        return (r, 0)

    def nxt_tile_in_seq(i):
        return (jnp.minimum(i + 1, n - 1) % (seq_len // tm), 0)

    return pl.pallas_call(
        functools.partial(_mixffn_kernel, seq_len=seq_len),
        grid=(n,),
        in_specs=[
            pl.BlockSpec((tm, D_MODEL), row),
            pl.BlockSpec((tm, ATTN_W), row),
            pl.BlockSpec((tm, 2 * CONV_W), first_tile),
            pl.BlockSpec((HALO, 2 * CONV_W), first_tile_next_halo),
            pl.BlockSpec((tm, POOL_W), first_tile),
            pl.BlockSpec((HALO, POOL_W), first_tile_next_halo),
            pl.BlockSpec((tm, POOL_W), first_tile),
            pl.BlockSpec((tm, 2 * CONV_W), nxt_tile),
            pl.BlockSpec((HALO, 2 * CONV_W), nxt_tile_prev_halo),
            pl.BlockSpec((HALO, 2 * CONV_W), nxt_tile_next_halo),
            pl.BlockSpec((tm, POOL_W), nxt_tile),
            pl.BlockSpec((HALO, POOL_W), nxt_tile_prev_halo),
            pl.BlockSpec((HALO, POOL_W), nxt_tile_next_halo),
            pl.BlockSpec((tm, POOL_W), nxt_tile_in_seq),
            _const_spec((CONV_KERNEL, SUBLANES, CONV_W)),
            _const_spec((1, CONV_W)),
            _const_spec((1, CONV_W)),
            _const_spec((1, CONV_W)),
            _const_spec((CONV_W, CONV_W)),
            _const_spec((POOL_W, POOL_W)),
            _const_spec((1, POOL_W)),
            _const_spec((D_MODEL, D_MODEL)),
            _const_spec((1, D_MODEL)),
            _const_spec((1, D_MODEL)),
            _const_spec((D_MODEL, D_FF)),
            _const_spec((D_MODEL, D_FF)),
            _const_spec((D_FF, D_MODEL)),
            _const_spec((1, D_MODEL)),
            _const_spec((1, D_MODEL)),
        ],
        out_specs=pl.BlockSpec((tm, D_MODEL), row),
        out_shape=jax.ShapeDtypeStruct((T, D_MODEL), F32),
        scratch_shapes=[
            pltpu.VMEM((2, tm, CONV_W + POOL_W), BF16),
            pltpu.VMEM((tm + 2 * HALO, CONV_W), F32),
            pltpu.VMEM((SUBLANES - 1, tm + 2 * HALO - SUBLANES, CONV_W), F32),
            pltpu.VMEM((tm + 2 * HALO, POOL_W), F32),
            pltpu.VMEM((tm, CONV_W), BF16),
            pltpu.VMEM((tm, D_FF), BF16),
        ],
        compiler_params=pltpu.CompilerParams(
            dimension_semantics=("arbitrary",), vmem_limit_bytes=VMEM_LIMIT),
        name="mixffn",
    )(x, ya, uconv, uconv, upool, upool, pool_cnt, uconv, uconv, uconv, upool, upool, upool, pool_cnt,
      w_dw, b_dw, g_cn, b_cn, w_pw, w_pool_bd, pool_scale,
      w_out, ln1_g, ln1_b, w_gate, w_up, w_down, ln2_g, ln2_b)


def _rope_tables(S):
    pos = jnp.arange(S, dtype=F32)
    inv_freq = 1.0 / (ROPE_BASE ** (jnp.arange(0, QK_ROPE_DIM, 2, dtype=F32) / QK_ROPE_DIM))
    ang = pos[:, None] * inv_freq[None, :]
    cos, sin = jnp.cos(ang), jnp.sin(ang)
    zeros = jnp.zeros((S, LANES - QK_ROPE_DIM), F32)
    cosf = jnp.concatenate([cos, cos, zeros], axis=-1)
    sinf = jnp.concatenate([-sin, sin, zeros], axis=-1)
    return cosf, sinf


def _pool_counts(S):
    t = jnp.arange(S)
    cols = []
    for w in POOL_WINDOWS:
        left = w // 2
        right = w - 1 - left
        cnt = (jnp.minimum(t + right, S - 1) - jnp.maximum(t - left, 0) + 1).astype(F32)
        cols.append(jnp.broadcast_to(cnt[:, None], (S, POOL_GW)))
    return jnp.concatenate(cols, axis=-1)


def _prepare_layer(w_in, w_pw, w_pool, w_uq, w_ukv, w_out, w_gate, w_up, w_down):
    w_in_p = jnp.pad(w_in, ((0, 0), (0, D_IN_PAD - D_IN))).astype(BF16)
    wq = w_uq.reshape(Q_LORA_RANK, N_HEADS, QK_HEAD_DIM)
    wq_nope = wq[:, :, :QK_NOPE_DIM].reshape(Q_LORA_RANK, N_HEADS * QK_NOPE_DIM)
    wq_rope = jnp.pad(wq[:, :, QK_NOPE_DIM:], ((0, 0), (0, 0), (0, LANES - QK_ROPE_DIM)))
    w_uq_p = jnp.concatenate([wq_nope, wq_rope.reshape(Q_LORA_RANK, N_HEADS * LANES)], axis=-1).astype(BF16)
    wkv = w_ukv.reshape(KV_LORA_RANK, N_HEADS, QK_NOPE_DIM + V_HEAD_DIM)
    w_ukv_p = jnp.concatenate(
        [wkv[:, :, :QK_NOPE_DIM].reshape(KV_LORA_RANK, N_HEADS * QK_NOPE_DIM),
         wkv[:, :, QK_NOPE_DIM:].reshape(KV_LORA_RANK, N_HEADS * V_HEAD_DIM)], axis=-1).astype(BF16)
    w_pool_bd = jnp.zeros((POOL_W, POOL_W), F32)
    for g in range(len(POOL_WINDOWS)):
        w_pool_bd = w_pool_bd.at[g * POOL_GW:(g + 1) * POOL_GW, g * POOL_GW:(g + 1) * POOL_GW].set(w_pool[g])
    return (w_in_p, w_pw.astype(BF16), w_pool_bd.astype(BF16), w_uq_p, w_ukv_p,
            w_out.astype(BF16), w_gate.astype(BF16), w_up.astype(BF16), w_down.astype(BF16))


def _trunk(x, layers, small):
    B, S, _ = x.shape
    cosf, sinf = _rope_tables(S)
    pool_cnt = _pool_counts(S)
    for l in range(DEPTH):
        w_in_p, w_pw_b, w_pool_bd, w_uq_p, w_ukv_p, w_out_b, w_gate_b, w_up_b, w_down_b = layers[l]
        (w_dw, b_dw, g_cn, b_cn, pool_scale, g_q, g_kv, ln1_g, ln1_b, ln2_g, ln2_b) = small[l]
        uconv, upool, qT, k, vT = _prep_call(x, w_in_p, g_q, g_kv, w_uq_p, w_ukv_p, cosf, sinf)
        yattn = _attn_call(qT, k, vT)
        x = _mixffn_call(
            x.reshape(B * S, D_MODEL), yattn.reshape(B * S, ATTN_W),
            uconv.reshape(B * S, 2 * CONV_W), upool.reshape(B * S, POOL_W), pool_cnt, S,
            w_dw, b_dw, g_cn, b_cn, w_pw_b, w_pool_bd, pool_scale,
            w_out_b, ln1_g, ln1_b, w_gate_b, w_up_b, w_down_b, ln2_g, ln2_b).reshape(B, S, D_MODEL)
    return x


def kernel(x_prompt, x_sample, w_in, w_dw, b_dw, g_cn, b_cn, w_pw, w_pool, pool_scale, g_q, g_kv,
           w_uq, w_ukv, w_out, ln1_g, ln1_b, w_gate, w_up, w_down, ln2_g, ln2_b):
    layers = [_prepare_layer(w_in[l], w_pw[l], w_pool[l], w_uq[l], w_ukv[l], w_out[l],
                             w_gate[l], w_up[l], w_down[l]) for l in range(DEPTH)]

    def row(a, l):
        return a[l].reshape(1, -1)

    def sublane_rows(a):
        return jnp.broadcast_to(a[:, None, :], (a.shape[0], SUBLANES, a.shape[1]))

    small = [(sublane_rows(w_dw[l]), row(b_dw, l), row(g_cn, l), row(b_cn, l), row(pool_scale, l), row(g_q, l),
              row(g_kv, l), row(ln1_g, l), row(ln1_b, l), row(ln2_g, l), row(ln2_b, l))
             for l in range(DEPTH)]
    y_prompt = _trunk(x_prompt, layers, small)
    y_sample = _trunk(x_sample, layers, small)
    return (y_prompt, y_sample)
```

```python
import functools

import jax
import jax.numpy as jnp
from jax import lax
from jax.experimental import pallas as pl
from jax.experimental.pallas import tpu as pltpu

F32 = jnp.float32
BF16 = jnp.bfloat16

D_MODEL = 1024
DEPTH = 4
CONV_W = 256
CONV_KERNEL = 31
CONV_PAD = CONV_KERNEL // 2
POOL_W = 256
POOL_WINDOWS = (2, 4, 8, 16)
POOL_GW = POOL_W // len(POOL_WINDOWS)
ATTN_W = 512
N_HEADS = 4
V_HEAD_DIM = 128
QK_NOPE_DIM = 128
QK_ROPE_DIM = 64
QK_HEAD_DIM = QK_NOPE_DIM + QK_ROPE_DIM
Q_LORA_RANK = 256
KV_LORA_RANK = 128
ROPE_BASE = 10000.0
D_IN = 2 * CONV_W + POOL_W + Q_LORA_RANK + KV_LORA_RANK + QK_ROPE_DIM
D_FF = 2816
ALPHA = (2.0 * DEPTH) ** 0.25
LN_EPS = 1e-5
RMS_EPS = 1e-6
QK_SCALE_LOG2E = QK_HEAD_DIM ** -0.5 * 1.4426950408889634

LANES = 128
SUBLANES = 8
MXU_DIM = 256
D_IN_PAD = 5 * MXU_DIM
QK_PAD = 2 * LANES
V_ROWS = V_HEAD_DIM + 16
HALO = 16
VMEM_LIMIT = 56 * 1024 * 1024

TM_PREP = 1024
PREP_SUB = 256
CONV_CHUNK = 64
TQ = 512
TK = 512
ATTN_MAX_UNROLL = 5
TM_FFN = 512
FF_CHUNK = 256
TOKEN_LAG = 1


def _const_spec(shape):
    nd = len(shape)
    return pl.BlockSpec(shape, lambda *_: (0,) * nd, pipeline_mode=pl.Buffered(1))


def _swap_rope_halves(x):
    lane = lax.broadcasted_iota(jnp.int32, x.shape, 1)
    return jnp.where(lane < QK_ROPE_DIM // 2,
                     pltpu.roll(x, LANES - QK_ROPE_DIM // 2, 1),
                     pltpu.roll(x, QK_ROPE_DIM // 2, 1))


def _prep_kernel(x_ref, w_in_ref, g_q_ref, g_kv_ref, w_uqT_ref, w_uk_ref, w_uvT_ref,
                 cos_ref, sin_ref, cosT_ref, sinT_ref,
                 uconv_ref, upool_ref, qT_ref, k_ref, vT_ref):
    tm = x_ref.shape[1]
    o_q = 2 * CONV_W + POOL_W
    o_kv = o_q + Q_LORA_RANK
    o_kr = o_kv + KV_LORA_RANK
    half = QK_ROPE_DIM // 2
    rope0 = N_HEADS * QK_NOPE_DIM

    def in_proj(rows):
        xb = x_ref[0, rows, :].astype(BF16)
        return jnp.dot(xb, w_in_ref[...], preferred_element_type=F32)

    def finish(rows, u):
        sub = rows.stop - rows.start
        uconv_ref[0, rows, :] = u[:, :2 * CONV_W]
        upool_ref[0, rows, :] = u[:, 2 * CONV_W:o_q]
        c_q = u[:, o_q:o_kv]
        c_kv = u[:, o_kv:o_kr]
        kr = u[:, o_kr:o_kr + LANES]
        ms_q = jnp.mean(c_q * c_q, axis=-1, keepdims=True)
        cqn = c_q * lax.rsqrt(ms_q + RMS_EPS) * g_q_ref[...]
        ms_kv = jnp.mean(c_kv * c_kv, axis=-1, keepdims=True)
        ckvn = c_kv * lax.rsqrt(ms_kv + RMS_EPS) * g_kv_ref[...]
        qT = jnp.dot(w_uqT_ref[...], cqn.T.astype(BF16), preferred_element_type=F32)
        vT = jnp.dot(w_uvT_ref[...], ckvn.T.astype(BF16), preferred_element_type=F32)
        kn = jnp.dot(ckvn.astype(BF16), w_uk_ref[...], preferred_element_type=F32)
        cosf = cos_ref[rows, :]
        sinf = sin_ref[rows, :]
        kro = (kr * cosf + _swap_rope_halves(kr) * sinf).astype(BF16)
        cosT = cosT_ref[:, rows]
        sinT = sinT_ref[:, rows]
        chunk, off = rows.start // TK, rows.start % TK
        for h in range(N_HEADS):
            nope = qT[h * QK_NOPE_DIM:(h + 1) * QK_NOPE_DIM]
            x1 = qT[rope0 + h * QK_ROPE_DIM:rope0 + h * QK_ROPE_DIM + half]
            x2 = qT[rope0 + h * QK_ROPE_DIM + half:rope0 + (h + 1) * QK_ROPE_DIM]
            qT_ref[0, h, 0:QK_NOPE_DIM, rows] = (nope * QK_SCALE_LOG2E).astype(BF16)
            qT_ref[0, h, QK_NOPE_DIM:QK_NOPE_DIM + half, rows] = (
                (x1 * cosT - x2 * sinT) * QK_SCALE_LOG2E).astype(BF16)
            qT_ref[0, h, QK_NOPE_DIM + half:QK_HEAD_DIM, rows] = (
                (x1 * sinT + x2 * cosT) * QK_SCALE_LOG2E).astype(BF16)
            qT_ref[0, h, QK_HEAD_DIM:QK_PAD, rows] = jnp.zeros((QK_PAD - QK_HEAD_DIM, sub), BF16)
            k_ref[0, h, rows, 0:LANES] = kn[:, h * LANES:(h + 1) * LANES].astype(BF16)
            k_ref[0, h, rows, LANES:QK_PAD] = kro
            vT_ref[0, h, chunk, 0:V_HEAD_DIM, off:off + sub] = (
                vT[h * V_HEAD_DIM:(h + 1) * V_HEAD_DIM].astype(BF16))
            vT_ref[0, h, chunk, V_HEAD_DIM:V_ROWS, off:off + sub] = jnp.ones(
                (V_ROWS - V_HEAD_DIM, sub), BF16)

    subs = [slice(r, r + PREP_SUB) for r in range(0, tm, PREP_SUB)]
    u_prev = in_proj(subs[0])
    for n in range(1, len(subs)):
        u_next = in_proj(subs[n])
        finish(subs[n - 1], u_prev)
        u_prev = u_next
    finish(subs[-1], u_prev)


def _prep_call(x, w_in, g_q, g_kv, w_uqT, w_uk, w_uvT, cosf, sinf, cosT, sinT):
    B, S, _ = x.shape
    tm = TM_PREP
    assert tm % TK == 0 and TK % PREP_SUB == 0
    grid = (B, S // tm)
    return pl.pallas_call(
        _prep_kernel,
        grid=grid,
        in_specs=[
            pl.BlockSpec((1, tm, D_MODEL), lambda b, i: (b, i, 0)),
            _const_spec((D_MODEL, D_IN_PAD)),
            _const_spec((1, Q_LORA_RANK)),
            _const_spec((1, KV_LORA_RANK)),
            _const_spec((N_HEADS * QK_HEAD_DIM, Q_LORA_RANK)),
            _const_spec((KV_LORA_RANK, N_HEADS * QK_NOPE_DIM)),
            _const_spec((N_HEADS * V_HEAD_DIM, KV_LORA_RANK)),
            pl.BlockSpec((tm, LANES), lambda b, i: (i, 0)),
            pl.BlockSpec((tm, LANES), lambda b, i: (i, 0)),
            pl.BlockSpec((QK_ROPE_DIM // 2, tm), lambda b, i: (0, i)),
            pl.BlockSpec((QK_ROPE_DIM // 2, tm), lambda b, i: (0, i)),
        ],
        out_specs=[
            pl.BlockSpec((1, tm, 2 * CONV_W), lambda b, i: (b, i, 0)),
            pl.BlockSpec((1, tm, POOL_W), lambda b, i: (b, i, 0)),
            pl.BlockSpec((1, N_HEADS, QK_PAD, tm), lambda b, i: (b, 0, 0, i)),
            pl.BlockSpec((1, N_HEADS, tm, QK_PAD), lambda b, i: (b, 0, i, 0)),
            pl.BlockSpec((1, N_HEADS, tm // TK, V_ROWS, TK), lambda b, i: (b, 0, i, 0, 0)),
        ],
        out_shape=[
            jax.ShapeDtypeStruct((B, S, 2 * CONV_W), F32),
            jax.ShapeDtypeStruct((B, S, POOL_W), F32),
            jax.ShapeDtypeStruct((B, N_HEADS, QK_PAD, S), BF16),
            jax.ShapeDtypeStruct((B, N_HEADS, S, QK_PAD), BF16),
            jax.ShapeDtypeStruct((B, N_HEADS, S // TK, V_ROWS, TK), BF16),
        ],
        compiler_params=pltpu.CompilerParams(
            dimension_semantics=("arbitrary", "arbitrary"), vmem_limit_bytes=VMEM_LIMIT),
        name="prep",
    )(x, w_in, g_q, g_kv, w_uqT, w_uk, w_uvT, cosf, sinf, cosT, sinT)


def _attn_kernel(qT_ref, k_ref, vT_ref, qT_next_ref, k0_next_ref, o_ref,
                 sa_ref, sb_ref, cmax_ref, acc_ref, *, seq_len, prefetch_next):
    tq = qT_ref.shape[3]
    nk = seq_len // TK
    qT = qT_ref[0, 0]

    def store_scores(k_chunk, q_t, s_ref):
        s = jnp.dot(k_chunk, q_t, preferred_element_type=F32)
        s_ref[:, 0:tq] = s
        return jnp.max(s, axis=0, keepdims=True)

    def scores(j, s_ref):
        start = pl.multiple_of(j * TK, TK)
        return store_scores(k_ref[0, 0, pl.ds(start, TK), :], qT, s_ref)

    def update(j, s_ref, cmax, m):
        m_new = jnp.maximum(m, cmax)
        alpha = jnp.exp2(m - m_new)
        p = jnp.exp2(s_ref[:, 0:tq] - m_new)
        pv = jnp.dot(vT_ref[0, 0, j], p.astype(BF16), preferred_element_type=F32)
        acc_ref[...] = acc_ref[...] * alpha + pv
        return m_new

    if prefetch_next:
        first_step = (pl.program_id(0) == 0) & (pl.program_id(1) == 0) & (pl.program_id(2) == 0)

        @pl.when(first_step)
        def _():
            cmax_ref[...] = scores(0, sa_ref)

        cmax_a = cmax_ref[...]
    else:
        cmax_a = scores(0, sa_ref)
    acc_ref[...] = jnp.zeros_like(acc_ref)
    m = jnp.full((1, tq), -jnp.inf, F32)

    def body(i, carry):
        m, cmax_a = carry
        cmax_b = scores(2 * i + 1, sb_ref)
        m = update(2 * i, sa_ref, cmax_a, m)
        cmax_a = scores(2 * i + 2, sa_ref)
        m = update(2 * i + 1, sb_ref, cmax_b, m)
        return m, cmax_a

    trips = nk // 2 - 1
    unroll = max(u for u in range(1, ATTN_MAX_UNROLL + 1) if trips % u == 0)
    m, cmax_a = lax.fori_loop(0, trips, body, (m, cmax_a), unroll=unroll)
    cmax_b = scores(nk - 1, sb_ref)
    m = update(nk - 2, sa_ref, cmax_a, m)
    if prefetch_next:
        cmax_ref[...] = store_scores(k0_next_ref[0, 0], qT_next_ref[0, 0], sa_ref)
    m = update(nk - 1, sb_ref, cmax_b, m)
    l = acc_ref[V_HEAD_DIM:V_HEAD_DIM + 1, :]
    o_ref[0] = (acc_ref[0:V_HEAD_DIM, :] / l).T.astype(BF16)


def _attn_call(qT, k, vT):
    B, H, S, _ = k.shape
    tq = TQ
    nq = S // tq
    assert (S // TK) % 2 == 0

    def next_step(b, h, i):
        wrap_i = i == nq - 1
        wrap_h = wrap_i & (h == H - 1)
        i2 = jnp.where(wrap_i, 0, i + 1)
        h2 = jnp.where(wrap_h, 0, jnp.where(wrap_i, h + 1, h))
        b2 = jnp.where(wrap_h, jnp.minimum(b + 1, B - 1), b)
        return b2, h2, i2

    prefetch_next = (S // TK) // 2 - 1 <= ATTN_MAX_UNROLL

    def next_q(b, h, i):
        if not prefetch_next:
            return (0, 0, 0, 0)
        b2, h2, i2 = next_step(b, h, i)
        return (b2, h2, 0, i2)

    def next_k0(b, h, i):
        if not prefetch_next:
            return (0, 0, 0, 0)
        b2, h2, _ = next_step(b, h, i)
        return (b2, h2, 0, 0)

    return pl.pallas_call(
        functools.partial(_attn_kernel, seq_len=S, prefetch_next=prefetch_next),
        grid=(B, H, nq),
        in_specs=[
            pl.BlockSpec((1, 1, QK_PAD, tq), lambda b, h, i: (b, h, 0, i)),
            pl.BlockSpec((1, 1, S, QK_PAD), lambda b, h, i: (b, h, 0, 0)),
            pl.BlockSpec((1, 1, S // TK, V_ROWS, TK), lambda b, h, i: (b, h, 0, 0, 0)),
            pl.BlockSpec((1, 1, QK_PAD, tq), next_q),
            pl.BlockSpec((1, 1, TK, QK_PAD), next_k0),
        ],
        out_specs=pl.BlockSpec((1, tq, V_HEAD_DIM), lambda b, h, i: (b, i, h)),
        out_shape=jax.ShapeDtypeStruct((B, S, ATTN_W), BF16),
        scratch_shapes=[
            pltpu.VMEM((TK, tq + LANES), F32),
            pltpu.VMEM((TK, tq + LANES), F32),
            pltpu.VMEM((1, tq), F32),
            pltpu.VMEM((V_ROWS, tq), F32),
        ],
        compiler_params=pltpu.CompilerParams(
            dimension_semantics=("arbitrary", "arbitrary", "arbitrary"),
            vmem_limit_bytes=VMEM_LIMIT),
        name="attn",
    )(qT, k, vT, qT, k)


def _glu(u):
    return u[:, :CONV_W] * jax.nn.sigmoid(u[:, CONV_W:])


def _zero_bits(x):
    bits = pltpu.bitcast(x, jnp.uint32)
    z = lax.shift_right_logical(lax.shift_right_logical(bits, jnp.uint32(16)), jnp.uint32(16))
    z = functools.reduce(
        jnp.bitwise_or, [z[r:r + SUBLANES] for r in range(0, x.shape[0], SUBLANES)])
    return pltpu.bitcast(z, jnp.int32).astype(F32)


def _add_token(x, token):
    r, c = token.shape
    top = x[0:r, 0:c] + token
    top = jnp.concatenate([top, x[0:r, c:]], axis=1) if c < x.shape[1] else top
    return jnp.concatenate([top, x[r:]], axis=0)


def _convpool_stages(uc_ref, uc_prev_ref, uc_next_ref, up_ref, up_prev_ref, up_next_ref,
                     has_prev, has_next, cnt_ref,
                     w_dw_ref, b_dw_ref, g_cn_ref, b_cn_ref,
                     hs_ref, hsh_ref, ps_ref, hn_ref, pd_ref):
    tm = uc_ref.shape[0]
    shift_rows = hsh_ref.shape[1]

    def halo_stage():
        if uc_prev_ref is None:
            hs_ref[0:HALO, :] = jnp.zeros((HALO, CONV_W), F32)
            ps_ref[0:HALO, :] = jnp.zeros((HALO, POOL_W), F32)
        else:
            hs_ref[0:HALO, :] = _glu(uc_prev_ref[...]) * has_prev
            ps_ref[0:HALO, :] = up_prev_ref[...] * has_prev
        hs_ref[HALO:HALO + tm, :] = _glu(uc_ref[...])
        hs_ref[HALO + tm:2 * HALO + tm, :] = _glu(uc_next_ref[...]) * has_next
        ps_ref[HALO:HALO + tm, :] = up_ref[...]
        ps_ref[HALO + tm:2 * HALO + tm, :] = up_next_ref[...] * has_next

    def shift_stage():
        for s in range(1, SUBLANES):
            hsh_ref[s - 1, 0:shift_rows, :] = hs_ref[s:s + shift_rows, :]

    def tap(c, k):
        off = HALO - CONV_PAD + k
        s, row = off % SUBLANES, c * CONV_CHUNK + off - off % SUBLANES
        if s == 0:
            return hs_ref[row:row + CONV_CHUNK, :]
        return hsh_ref[s - 1, row:row + CONV_CHUNK, :]

    def conv_stage(c):
        def run():
            groups = (CONV_CHUNK // SUBLANES, SUBLANES, CONV_W)
            acc = jnp.broadcast_to(b_dw_ref[...], groups)
            for k in range(CONV_KERNEL):
                acc = acc + tap(c, k).reshape(groups) * w_dw_ref[k]
            acc = acc.reshape(CONV_CHUNK, CONV_W)
            mu = jnp.mean(acc, axis=-1, keepdims=True)
            xc = acc - mu
            var = jnp.mean(xc * xc, axis=-1, keepdims=True)
            hn = xc * lax.rsqrt(var + LN_EPS) * g_cn_ref[...] + b_cn_ref[...]
            act = hn * jax.nn.sigmoid(hn)
            hn_ref[c * CONV_CHUNK:(c + 1) * CONV_CHUNK, :] = act.astype(BF16)
            return _zero_bits(act)
        return run

    def pool_stage():
        def shifted(d, lo, hi):
            return ps_ref[HALO + d:HALO + d + tm, lo:hi]

        lane = lax.broadcasted_iota(jnp.int32, (tm, LANES), 1)
        first = lane < POOL_GW
        s2 = shifted(-1, 0, LANES) + shifted(0, 0, LANES)
        s4 = s2 + shifted(-2, 0, LANES) + shifted(1, 0, LANES)
        mean_lo = jnp.where(first, s2, s4) / cnt_ref[:, 0:LANES]
        s8 = shifted(-4, LANES, 2 * LANES)
        for d in range(-3, 4):
            s8 = s8 + shifted(d, LANES, 2 * LANES)
        s16 = s8
        for d in (-8, -7, -6, -5, 4, 5, 6, 7):
            s16 = s16 + shifted(d, LANES, 2 * LANES)
        mean_hi = jnp.where(first, s8, s16) / cnt_ref[:, LANES:2 * LANES]
        d_lo = mean_lo - shifted(0, 0, LANES)
        d_hi = mean_hi - shifted(0, LANES, 2 * LANES)
        pd_ref[:, 0:LANES] = d_lo.astype(BF16)
        pd_ref[:, LANES:2 * LANES] = d_hi.astype(BF16)
        return _zero_bits(jnp.concatenate([d_lo, d_hi], axis=-1))

    return [halo_stage, pool_stage, shift_stage] + [conv_stage(c) for c in range(tm // CONV_CHUNK)]


def _convpool_outputs(hn_ref, pd_ref, w_pw_ref, w_pool_ref, pscale_ref):
    y_conv = jnp.dot(hn_ref[...], w_pw_ref[...], preferred_element_type=F32)
    y_pool = jnp.dot(pd_ref[...], w_pool_ref[...], preferred_element_type=F32) * pscale_ref[...]
    return y_conv.astype(BF16), y_pool.astype(BF16)


def _layer_norm(x, g, b):
    mu = jnp.mean(x, axis=-1, keepdims=True)
    xc = x - mu
    var = jnp.mean(xc * xc, axis=-1, keepdims=True)
    return xc * lax.rsqrt(var + LN_EPS) * g + b


def _mixffn_kernel(x_ref, ya_ref,
                   uc0_ref, uc0_next_ref, up0_ref, up0_next_ref, cnt0_ref,
                   uc_ref, uc_prev_ref, uc_next_ref, up_ref, up_prev_ref, up_next_ref, cnt_ref,
                   w_dw_ref, b_dw_ref, g_cn_ref, b_cn_ref, w_pw_ref, w_pool_ref, pscale_ref,
                   w_out_ref, ln1_g_ref, ln1_b_ref, w_gate_ref, w_up_ref, w_down_ref,
                   ln2_g_ref, ln2_b_ref,
                   o_ref, hs_ref, hsh_ref, ps_ref, hn_ref, pd_ref, h_ref, *, seq_len):
    tm = x_ref.shape[0]
    tiles_per_seq = seq_len // tm
    i = pl.program_id(0)
    vector_params = (w_dw_ref, b_dw_ref, g_cn_ref, b_cn_ref, hs_ref, hsh_ref, ps_ref, hn_ref, pd_ref)

    @pl.when(i == 0)
    def _():
        for stage in _convpool_stages(uc0_ref, None, uc0_next_ref, up0_ref, None, up0_next_ref,
                                      0.0, 1.0 if tiles_per_seq > 1 else 0.0, cnt0_ref,
                                      *vector_params):
            stage()

    y_conv, y_pool = _convpool_outputs(hn_ref, pd_ref, w_pw_ref, w_pool_ref, pscale_ref)
    mix = jnp.dot(y_conv, w_out_ref[0:CONV_W, :], preferred_element_type=F32)
    mix += jnp.dot(y_pool, w_out_ref[CONV_W:CONV_W + POOL_W, :], preferred_element_type=F32)
    mix += jnp.dot(ya_ref[...], w_out_ref[CONV_W + POOL_W:, :], preferred_element_type=F32)
    x1 = _layer_norm(ALPHA * x_ref[...] + mix, ln1_g_ref[...], ln1_b_ref[...])
    x1b = x1.astype(BF16)

    j = jnp.minimum(i + 1, pl.num_programs(0) - 1) % tiles_per_seq
    next_stages = _convpool_stages(
        uc_ref, uc_prev_ref, uc_next_ref, up_ref, up_prev_ref, up_next_ref,
        (j > 0).astype(F32), (j < tiles_per_seq - 1).astype(F32), cnt_ref, *vector_params)
    n_chunks = D_FF // FF_CHUNK
    assert len(next_stages) <= n_chunks
    tokens = []
    for c in range(n_chunks):
        cols = slice(c * FF_CHUNK, (c + 1) * FF_CHUNK)
        g = jnp.dot(x1b, w_gate_ref[:, cols], preferred_element_type=F32)
        u = jnp.dot(x1b, w_up_ref[:, cols], preferred_element_type=F32)
        hc = g * jax.nn.sigmoid(g) * u
        token = tokens[c - TOKEN_LAG] if c >= TOKEN_LAG else None
        h_ref[:, cols] = (hc if token is None else _add_token(hc, token)).astype(BF16)
        tokens.append(next_stages[c]() if c < len(next_stages) else None)
    y = jnp.dot(h_ref[...], w_down_ref[...], preferred_element_type=F32)
    for token in tokens[n_chunks - TOKEN_LAG:]:
        if token is not None:
            y = _add_token(y, token)
    o_ref[...] = _layer_norm(ALPHA * x1 + y, ln2_g_ref[...], ln2_b_ref[...])


def _mixffn_call(x, ya, uconv, upool, pool_cnt, seq_len, w_dw, b_dw, g_cn, b_cn, w_pw, w_pool_bd, pool_scale,
                 w_out, ln1_g, ln1_b, w_gate, w_up, w_down, ln2_g, ln2_b):
    T = x.shape[0]
    tm = TM_FFN
    n = T // tm
    r = tm // HALO
    n_halo_blocks = T // HALO

    def row(i):
        return (i, 0)

    def nxt_tile(i):
        return (jnp.minimum(i + 1, n - 1), 0)

    def nxt_tile_prev_halo(i):
        return (jnp.maximum(jnp.minimum(i + 1, n - 1) * r - 1, 0), 0)

    def nxt_tile_next_halo(i):
        return (jnp.minimum((jnp.minimum(i + 1, n - 1) + 1) * r, n_halo_blocks - 1), 0)

    def first_tile(i):
        return (0, 0)

    def first_tile_next_halo(i):
        return (r, 0)

    def nxt_tile_in_seq(i):
        return (jnp.minimum(i + 1, n - 1) % (seq_len // tm), 0)

    return pl.pallas_call(
        functools.partial(_mixffn_kernel, seq_len=seq_len),
        grid=(n,),
        in_specs=[
            pl.BlockSpec((tm, D_MODEL), row),
            pl.BlockSpec((tm, ATTN_W), row),
            pl.BlockSpec((tm, 2 * CONV_W), first_tile),
            pl.BlockSpec((HALO, 2 * CONV_W), first_tile_next_halo),
            pl.BlockSpec((tm, POOL_W), first_tile),
            pl.BlockSpec((HALO, POOL_W), first_tile_next_halo),
            pl.BlockSpec((tm, POOL_W), first_tile),
            pl.BlockSpec((tm, 2 * CONV_W), nxt_tile),
            pl.BlockSpec((HALO, 2 * CONV_W), nxt_tile_prev_halo),
            pl.BlockSpec((HALO, 2 * CONV_W), nxt_tile_next_halo),
            pl.BlockSpec((tm, POOL_W), nxt_tile),
            pl.BlockSpec((HALO, POOL_W), nxt_tile_prev_halo),
            pl.BlockSpec((HALO, POOL_W), nxt_tile_next_halo),
            pl.BlockSpec((tm, POOL_W), nxt_tile_in_seq),
            _const_spec((CONV_KERNEL, SUBLANES, CONV_W)),
            _const_spec((1, CONV_W)),
            _const_spec((1, CONV_W)),
            _const_spec((1, CONV_W)),
            _const_spec((CONV_W, CONV_W)),
            _const_spec((POOL_W, POOL_W)),
            _const_spec((1, POOL_W)),
            _const_spec((D_MODEL, D_MODEL)),
            _const_spec((1, D_MODEL)),
            _const_spec((1, D_MODEL)),
            _const_spec((D_MODEL, D_FF)),
            _const_spec((D_MODEL, D_FF)),
            _const_spec((D_FF, D_MODEL)),
            _const_spec((1, D_MODEL)),
            _const_spec((1, D_MODEL)),
        ],
        out_specs=pl.BlockSpec((tm, D_MODEL), row),
        out_shape=jax.ShapeDtypeStruct((T, D_MODEL), F32),
        scratch_shapes=[
            pltpu.VMEM((tm + 2 * HALO, CONV_W), F32),
            pltpu.VMEM((SUBLANES - 1, tm + 2 * HALO - SUBLANES, CONV_W), F32),
            pltpu.VMEM((tm + 2 * HALO, POOL_W), F32),
            pltpu.VMEM((tm, CONV_W), BF16),
            pltpu.VMEM((tm, POOL_W), BF16),
            pltpu.VMEM((tm, D_FF), BF16),
        ],
        compiler_params=pltpu.CompilerParams(
            dimension_semantics=("arbitrary",), vmem_limit_bytes=VMEM_LIMIT),
        name="mixffn",
    )(x, ya, uconv, uconv, upool, upool, pool_cnt, uconv, uconv, uconv, upool, upool, upool, pool_cnt,
      w_dw, b_dw, g_cn, b_cn, w_pw, w_pool_bd, pool_scale,
      w_out, ln1_g, ln1_b, w_gate, w_up, w_down, ln2_g, ln2_b)


def _rope_tables(S):
    pos = jnp.arange(S, dtype=F32)
    inv_freq = 1.0 / (ROPE_BASE ** (jnp.arange(0, QK_ROPE_DIM, 2, dtype=F32) / QK_ROPE_DIM))
    ang = pos[:, None] * inv_freq[None, :]
    cos, sin = jnp.cos(ang), jnp.sin(ang)
    zeros = jnp.zeros((S, LANES - QK_ROPE_DIM), F32)
    cosf = jnp.concatenate([cos, cos, zeros], axis=-1)
    sinf = jnp.concatenate([-sin, sin, zeros], axis=-1)
    return cosf, sinf, cos.T, sin.T


def _pool_counts(S):
    t = jnp.arange(S)
    cols = []
    for w in POOL_WINDOWS:
        left = w // 2
        right = w - 1 - left
        cnt = (jnp.minimum(t + right, S - 1) - jnp.maximum(t - left, 0) + 1).astype(F32)
        cols.append(jnp.broadcast_to(cnt[:, None], (S, POOL_GW)))
    return jnp.concatenate(cols, axis=-1)


def _prepare_layer(w_in, w_pw, w_pool, w_uq, w_ukv, w_out, w_gate, w_up, w_down):
    w_in_p = jnp.pad(w_in, ((0, 0), (0, D_IN_PAD - D_IN))).astype(BF16)
    wq = w_uq.reshape(Q_LORA_RANK, N_HEADS, QK_HEAD_DIM)
    w_uqT = jnp.concatenate(
        [wq[:, :, :QK_NOPE_DIM].reshape(Q_LORA_RANK, N_HEADS * QK_NOPE_DIM),
         wq[:, :, QK_NOPE_DIM:].reshape(Q_LORA_RANK, N_HEADS * QK_ROPE_DIM)], axis=-1).T.astype(BF16)
    wkv = w_ukv.reshape(KV_LORA_RANK, N_HEADS, QK_NOPE_DIM + V_HEAD_DIM)
    w_uk = wkv[:, :, :QK_NOPE_DIM].reshape(KV_LORA_RANK, N_HEADS * QK_NOPE_DIM).astype(BF16)
    w_uvT = wkv[:, :, QK_NOPE_DIM:].reshape(KV_LORA_RANK, N_HEADS * V_HEAD_DIM).T.astype(BF16)
    w_pool_bd = jnp.zeros((POOL_W, POOL_W), F32)
    for g in range(len(POOL_WINDOWS)):
        w_pool_bd = w_pool_bd.at[g * POOL_GW:(g + 1) * POOL_GW, g * POOL_GW:(g + 1) * POOL_GW].set(w_pool[g])
    return (w_in_p, w_pw.astype(BF16), w_pool_bd.astype(BF16), w_uqT, w_uk, w_uvT,
            w_out.astype(BF16), w_gate.astype(BF16), w_up.astype(BF16), w_down.astype(BF16))


def _trunk(x, layers, small):
    B, S, _ = x.shape
    cosf, sinf, cosT, sinT = _rope_tables(S)
    pool_cnt = _pool_counts(S)
    for l in range(DEPTH):
        w_in_p, w_pw_b, w_pool_bd, w_uqT, w_uk, w_uvT, w_out_b, w_gate_b, w_up_b, w_down_b = layers[l]
        (w_dw, b_dw, g_cn, b_cn, pool_scale, g_q, g_kv, ln1_g, ln1_b, ln2_g, ln2_b) = small[l]
        uconv, upool, qT, k, vT = _prep_call(
            x, w_in_p, g_q, g_kv, w_uqT, w_uk, w_uvT, cosf, sinf, cosT, sinT)
        yattn = _attn_call(qT, k, vT)
        x = _mixffn_call(
            x.reshape(B * S, D_MODEL), yattn.reshape(B * S, ATTN_W),
            uconv.reshape(B * S, 2 * CONV_W), upool.reshape(B * S, POOL_W), pool_cnt, S,
            w_dw, b_dw, g_cn, b_cn, w_pw_b, w_pool_bd, pool_scale,
            w_out_b, ln1_g, ln1_b, w_gate_b, w_up_b, w_down_b, ln2_g, ln2_b).reshape(B, S, D_MODEL)
    return x


def kernel(x_prompt, x_sample, w_in, w_dw, b_dw, g_cn, b_cn, w_pw, w_pool, pool_scale, g_q, g_kv,
           w_uq, w_ukv, w_out, ln1_g, ln1_b, w_gate, w_up, w_down, ln2_g, ln2_b):
    layers = [_prepare_layer(w_in[l], w_pw[l], w_pool[l], w_uq[l], w_ukv[l], w_out[l],
                             w_gate[l], w_up[l], w_down[l]) for l in range(DEPTH)]

    def row(a, l):
        return a[l].reshape(1, -1)

    def sublane_rows(a):
        return jnp.broadcast_to(a[:, None, :], (a.shape[0], SUBLANES, a.shape[1]))

    small = [(sublane_rows(w_dw[l]), row(b_dw, l), row(g_cn, l), row(b_cn, l), row(pool_scale, l), row(g_q, l),
              row(g_kv, l), row(ln1_g, l), row(ln1_b, l), row(ln2_g, l), row(ln2_b, l))
             for l in range(DEPTH)]
    y_prompt = _trunk(x_prompt, layers, small)
    y_sample = _trunk(x_sample, layers, small)
    return (y_prompt, y_sample)
```

```python
import functools

import jax
import jax.numpy as jnp
from jax import lax
from jax.experimental import pallas as pl
from jax.experimental.pallas import tpu as pltpu

F32 = jnp.float32
BF16 = jnp.bfloat16

D_MODEL = 1024
DEPTH = 4
CONV_W = 256
CONV_KERNEL = 31
CONV_PAD = CONV_KERNEL // 2
POOL_W = 256
POOL_WINDOWS = (2, 4, 8, 16)
POOL_GW = POOL_W // len(POOL_WINDOWS)
ATTN_W = 512
N_HEADS = 4
V_HEAD_DIM = 128
QK_NOPE_DIM = 128
QK_ROPE_DIM = 64
QK_HEAD_DIM = QK_NOPE_DIM + QK_ROPE_DIM
Q_LORA_RANK = 256
KV_LORA_RANK = 128
ROPE_BASE = 10000.0
D_IN = 2 * CONV_W + POOL_W + Q_LORA_RANK + KV_LORA_RANK + QK_ROPE_DIM
D_FF = 2816
ALPHA = (2.0 * DEPTH) ** 0.25
LN_EPS = 1e-5
RMS_EPS = 1e-6
QK_SCALE_LOG2E = QK_HEAD_DIM ** -0.5 * 1.4426950408889634

LANES = 128
SUBLANES = 8
MXU_DIM = 256
D_IN_PAD = 5 * MXU_DIM
QK_PAD = 2 * LANES
V_ROWS = V_HEAD_DIM + 16
HALO = 16
VMEM_LIMIT = 56 * 1024 * 1024

TM_PREP = 1024
PREP_SUB = 256
CONV_CHUNK = 64
TQ = 512
TK = 512
ATTN_MAX_UNROLL = 5
ATTN_TILES_PER_STEP = 8
TM_FFN = 512
FF_CHUNK = 256
TOKEN_LAG = 1


def _const_spec(shape):
    nd = len(shape)
    return pl.BlockSpec(shape, lambda *_: (0,) * nd, pipeline_mode=pl.Buffered(1))


def _swap_rope_halves(x):
    lane = lax.broadcasted_iota(jnp.int32, x.shape, 1)
    return jnp.where(lane < QK_ROPE_DIM // 2,
                     pltpu.roll(x, LANES - QK_ROPE_DIM // 2, 1),
                     pltpu.roll(x, QK_ROPE_DIM // 2, 1))


def _prep_kernel(x_ref, w_in_ref, g_q_ref, g_kv_ref, w_uqT_ref, w_uk_ref, w_uvT_ref,
                 cos_ref, sin_ref, cosT_ref, sinT_ref,
                 uconv_ref, upool_ref, qT_ref, k_ref, vT_ref):
    tm = x_ref.shape[1]
    o_q = 2 * CONV_W + POOL_W
    o_kv = o_q + Q_LORA_RANK
    o_kr = o_kv + KV_LORA_RANK
    half = QK_ROPE_DIM // 2
    rope0 = N_HEADS * QK_NOPE_DIM

    def in_proj(rows):
        xb = x_ref[0, rows, :].astype(BF16)
        return jnp.dot(xb, w_in_ref[...], preferred_element_type=F32)

    def finish(rows, u):
        sub = rows.stop - rows.start
        uconv_ref[0, rows, :] = u[:, :2 * CONV_W]
        upool_ref[0, rows, :] = u[:, 2 * CONV_W:o_q]
        c_q = u[:, o_q:o_kv]
        c_kv = u[:, o_kv:o_kr]
        kr = u[:, o_kr:o_kr + LANES]
        ms_q = jnp.mean(c_q * c_q, axis=-1, keepdims=True)
        cqn = c_q * lax.rsqrt(ms_q + RMS_EPS) * g_q_ref[...]
        ms_kv = jnp.mean(c_kv * c_kv, axis=-1, keepdims=True)
        ckvn = c_kv * lax.rsqrt(ms_kv + RMS_EPS) * g_kv_ref[...]
        qT = jnp.dot(w_uqT_ref[...], cqn.T.astype(BF16), preferred_element_type=F32)
        vT = jnp.dot(w_uvT_ref[...], ckvn.T.astype(BF16), preferred_element_type=F32)
        kn = jnp.dot(ckvn.astype(BF16), w_uk_ref[...], preferred_element_type=F32)
        cosf = cos_ref[rows, :]
        sinf = sin_ref[rows, :]
        kro = (kr * cosf + _swap_rope_halves(kr) * sinf).astype(BF16)
        cosT = cosT_ref[:, rows]
        sinT = sinT_ref[:, rows]
        chunk, off = rows.start // TK, rows.start % TK
        qt, qoff = rows.start // TQ, rows.start % TQ
        qcols = slice(qoff, qoff + sub)
        for h in range(N_HEADS):
            nope = qT[h * QK_NOPE_DIM:(h + 1) * QK_NOPE_DIM]
            x1 = qT[rope0 + h * QK_ROPE_DIM:rope0 + h * QK_ROPE_DIM + half]
            x2 = qT[rope0 + h * QK_ROPE_DIM + half:rope0 + (h + 1) * QK_ROPE_DIM]
            qT_ref[0, h, qt, 0:QK_NOPE_DIM, qcols] = (nope * QK_SCALE_LOG2E).astype(BF16)
            qT_ref[0, h, qt, QK_NOPE_DIM:QK_NOPE_DIM + half, qcols] = (
                (x1 * cosT - x2 * sinT) * QK_SCALE_LOG2E).astype(BF16)
            qT_ref[0, h, qt, QK_NOPE_DIM + half:QK_HEAD_DIM, qcols] = (
                (x1 * sinT + x2 * cosT) * QK_SCALE_LOG2E).astype(BF16)
            qT_ref[0, h, qt, QK_HEAD_DIM:QK_PAD, qcols] = jnp.zeros((QK_PAD - QK_HEAD_DIM, sub), BF16)
            k_ref[0, h, rows, 0:LANES] = kn[:, h * LANES:(h + 1) * LANES].astype(BF16)
            k_ref[0, h, rows, LANES:QK_PAD] = kro
            vT_ref[0, h, chunk, 0:V_HEAD_DIM, off:off + sub] = (
                vT[h * V_HEAD_DIM:(h + 1) * V_HEAD_DIM].astype(BF16))
            vT_ref[0, h, chunk, V_HEAD_DIM:V_ROWS, off:off + sub] = jnp.ones(
                (V_ROWS - V_HEAD_DIM, sub), BF16)

    subs = [slice(r, r + PREP_SUB) for r in range(0, tm, PREP_SUB)]
    u_prev = in_proj(subs[0])
    for n in range(1, len(subs)):
        u_next = in_proj(subs[n])
        finish(subs[n - 1], u_prev)
        u_prev = u_next
    finish(subs[-1], u_prev)


def _prep_call(x, w_in, g_q, g_kv, w_uqT, w_uk, w_uvT, cosf, sinf, cosT, sinT):
    B, S, _ = x.shape
    tm = TM_PREP
    assert tm % TK == 0 and TK % PREP_SUB == 0 and tm % TQ == 0 and TQ % PREP_SUB == 0
    grid = (B, S // tm)
    return pl.pallas_call(
        _prep_kernel,
        grid=grid,
        in_specs=[
            pl.BlockSpec((1, tm, D_MODEL), lambda b, i: (b, i, 0)),
            _const_spec((D_MODEL, D_IN_PAD)),
            _const_spec((1, Q_LORA_RANK)),
            _const_spec((1, KV_LORA_RANK)),
            _const_spec((N_HEADS * QK_HEAD_DIM, Q_LORA_RANK)),
            _const_spec((KV_LORA_RANK, N_HEADS * QK_NOPE_DIM)),
            _const_spec((N_HEADS * V_HEAD_DIM, KV_LORA_RANK)),
            pl.BlockSpec((tm, LANES), lambda b, i: (i, 0)),
            pl.BlockSpec((tm, LANES), lambda b, i: (i, 0)),
            pl.BlockSpec((QK_ROPE_DIM // 2, tm), lambda b, i: (0, i)),
            pl.BlockSpec((QK_ROPE_DIM // 2, tm), lambda b, i: (0, i)),
        ],
        out_specs=[
            pl.BlockSpec((1, tm, 2 * CONV_W), lambda b, i: (b, i, 0)),
            pl.BlockSpec((1, tm, POOL_W), lambda b, i: (b, i, 0)),
            pl.BlockSpec((1, N_HEADS, tm // TQ, QK_PAD, TQ), lambda b, i: (b, 0, i, 0, 0)),
            pl.BlockSpec((1, N_HEADS, tm, QK_PAD), lambda b, i: (b, 0, i, 0)),
            pl.BlockSpec((1, N_HEADS, tm // TK, V_ROWS, TK), lambda b, i: (b, 0, i, 0, 0)),
        ],
        out_shape=[
            jax.ShapeDtypeStruct((B, S, 2 * CONV_W), F32),
            jax.ShapeDtypeStruct((B, S, POOL_W), F32),
            jax.ShapeDtypeStruct((B, N_HEADS, S // TQ, QK_PAD, TQ), BF16),
            jax.ShapeDtypeStruct((B, N_HEADS, S, QK_PAD), BF16),
            jax.ShapeDtypeStruct((B, N_HEADS, S // TK, V_ROWS, TK), BF16),
        ],
        compiler_params=pltpu.CompilerParams(
            dimension_semantics=("arbitrary", "arbitrary"), vmem_limit_bytes=VMEM_LIMIT),
        name="prep",
    )(x, w_in, g_q, g_kv, w_uqT, w_uk, w_uvT, cosf, sinf, cosT, sinT)


def _attn_kernel(qT_ref, k_ref, vT_ref, o_ref, sa_ref, sb_ref, acc_ref, *, seq_len):
    n_tiles, _, tq = qT_ref.shape[2:]
    nk = seq_len // TK

    def scores(j, q_t, s_ref):
        start = pl.multiple_of(j * TK, TK)
        s = jnp.dot(k_ref[0, 0, pl.ds(start, TK), :], q_t, preferred_element_type=F32)
        s_ref[:, 0:tq] = s
        return jnp.max(s, axis=0, keepdims=True)

    def update(j, s_ref, cmax, m):
        m_new = jnp.maximum(m, cmax)
        alpha = jnp.exp2(m - m_new)
        p = jnp.exp2(s_ref[:, 0:tq] - m_new)
        pv = jnp.dot(vT_ref[0, 0, j], p.astype(BF16), preferred_element_type=F32)
        acc_ref[...] = acc_ref[...] * alpha + pv
        return m_new

    def tile(g, cmax_a):
        qT = qT_ref[0, 0, g]
        acc_ref[...] = jnp.zeros_like(acc_ref)
        m = jnp.full((1, tq), -jnp.inf, F32)

        def body(i, carry):
            m, cmax_a = carry
            cmax_b = scores(2 * i + 1, qT, sb_ref)
            m = update(2 * i, sa_ref, cmax_a, m)
            cmax_a = scores(2 * i + 2, qT, sa_ref)
            m = update(2 * i + 1, sb_ref, cmax_b, m)
            return m, cmax_a

        trips = nk // 2 - 1
        unroll = max(u for u in range(1, ATTN_MAX_UNROLL + 1) if trips % u == 0)
        m, cmax_a = lax.fori_loop(0, trips, body, (m, cmax_a), unroll=unroll)
        cmax_b = scores(nk - 1, qT, sb_ref)
        m = update(nk - 2, sa_ref, cmax_a, m)
        cmax_next = scores(0, qT_ref[0, 0, jnp.minimum(g + 1, n_tiles - 1)], sa_ref)
        m = update(nk - 1, sb_ref, cmax_b, m)
        l = acc_ref[V_HEAD_DIM:V_HEAD_DIM + 1, :]
        rows = pl.ds(pl.multiple_of(g * tq, tq), tq)
        o_ref[0, rows, :] = (acc_ref[0:V_HEAD_DIM, :] / l).T.astype(BF16)
        return cmax_next

    lax.fori_loop(0, n_tiles, tile, scores(0, qT_ref[0, 0, 0], sa_ref))


def _attn_call(qT, k, vT):
    B, H, S, _ = k.shape
    tq = TQ
    tiles = min(S // tq, ATTN_TILES_PER_STEP)
    assert (S // TK) % 2 == 0 and (S // tq) % tiles == 0
    return pl.pallas_call(
        functools.partial(_attn_kernel, seq_len=S),
        grid=(B, H, S // (tq * tiles)),
        in_specs=[
            pl.BlockSpec((1, 1, tiles, QK_PAD, tq), lambda b, h, i: (b, h, i, 0, 0)),
            pl.BlockSpec((1, 1, S, QK_PAD), lambda b, h, i: (b, h, 0, 0)),
            pl.BlockSpec((1, 1, S // TK, V_ROWS, TK), lambda b, h, i: (b, h, 0, 0, 0)),
        ],
        out_specs=pl.BlockSpec((1, tiles * tq, V_HEAD_DIM), lambda b, h, i: (b, i, h)),
        out_shape=jax.ShapeDtypeStruct((B, S, ATTN_W), BF16),
        scratch_shapes=[
            pltpu.VMEM((TK, tq + LANES), F32),
            pltpu.VMEM((TK, tq + LANES), F32),
            pltpu.VMEM((V_ROWS, tq), F32),
        ],
        compiler_params=pltpu.CompilerParams(
            dimension_semantics=("arbitrary", "arbitrary", "arbitrary"),
            vmem_limit_bytes=VMEM_LIMIT),
        name="attn",
    )(qT, k, vT)


def _glu(u):
    return u[:, :CONV_W] * jax.nn.sigmoid(u[:, CONV_W:])


def _zero_bits(x):
    bits = pltpu.bitcast(x, jnp.uint32)
    z = lax.shift_right_logical(lax.shift_right_logical(bits, jnp.uint32(16)), jnp.uint32(16))
    z = functools.reduce(
        jnp.bitwise_or, [z[r:r + SUBLANES] for r in range(0, x.shape[0], SUBLANES)])
    return pltpu.bitcast(z, jnp.int32).astype(F32)


def _add_token(x, token):
    r, c = token.shape
    top = x[0:r, 0:c] + token
    top = jnp.concatenate([top, x[0:r, c:]], axis=1) if c < x.shape[1] else top
    return jnp.concatenate([top, x[r:]], axis=0)


def _convpool_stages(uc_ref, uc_prev_ref, uc_next_ref, up_ref, up_prev_ref, up_next_ref,
                     has_prev, has_next, cnt_ref,
                     w_dw_ref, b_dw_ref, g_cn_ref, b_cn_ref,
                     hs_ref, hsh_ref, ps_ref, hn_ref, pd_ref):
    tm = uc_ref.shape[0]
    shift_rows = hsh_ref.shape[1]

    def halo_stage():
        if uc_prev_ref is None:
            hs_ref[0:HALO, :] = jnp.zeros((HALO, CONV_W), F32)
            ps_ref[0:HALO, :] = jnp.zeros((HALO, POOL_W), F32)
        else:
            hs_ref[0:HALO, :] = _glu(uc_prev_ref[...]) * has_prev
            ps_ref[0:HALO, :] = up_prev_ref[...] * has_prev
        hs_ref[HALO:HALO + tm, :] = _glu(uc_ref[...])
        hs_ref[HALO + tm:2 * HALO + tm, :] = _glu(uc_next_ref[...]) * has_next
        ps_ref[HALO:HALO + tm, :] = up_ref[...]
        ps_ref[HALO + tm:2 * HALO + tm, :] = up_next_ref[...] * has_next

    def shift_stage():
        for s in range(1, SUBLANES):
            hsh_ref[s - 1, 0:shift_rows, :] = hs_ref[s:s + shift_rows, :]

    def tap(c, k):
        off = HALO - CONV_PAD + k
        s, row = off % SUBLANES, c * CONV_CHUNK + off - off % SUBLANES
        if s == 0:
            return hs_ref[row:row + CONV_CHUNK, :]
        return hsh_ref[s - 1, row:row + CONV_CHUNK, :]

    def conv_stage(c):
        def run():
            groups = (CONV_CHUNK // SUBLANES, SUBLANES, CONV_W)
            acc = jnp.broadcast_to(b_dw_ref[...], groups)
            for k in range(CONV_KERNEL):
                acc = acc + tap(c, k).reshape(groups) * w_dw_ref[k]
            acc = acc.reshape(CONV_CHUNK, CONV_W)
            mu = jnp.mean(acc, axis=-1, keepdims=True)
            xc = acc - mu
            var = jnp.mean(xc * xc, axis=-1, keepdims=True)
            hn = xc * lax.rsqrt(var + LN_EPS) * g_cn_ref[...] + b_cn_ref[...]
            act = hn * jax.nn.sigmoid(hn)
            hn_ref[c * CONV_CHUNK:(c + 1) * CONV_CHUNK, :] = act.astype(BF16)
            return _zero_bits(act)
        return run

    def pool_stage():
        def shifted(d, lo, hi):
            return ps_ref[HALO + d:HALO + d + tm, lo:hi]

        lane = lax.broadcasted_iota(jnp.int32, (tm, LANES), 1)
        first = lane < POOL_GW
        s2 = shifted(-1, 0, LANES) + shifted(0, 0, LANES)
        s4 = s2 + shifted(-2, 0, LANES) + shifted(1, 0, LANES)
        mean_lo = jnp.where(first, s2, s4) / cnt_ref[:, 0:LANES]
        s8 = shifted(-4, LANES, 2 * LANES)
        for d in range(-3, 4):
            s8 = s8 + shifted(d, LANES, 2 * LANES)
        s16 = s8
        for d in (-8, -7, -6, -5, 4, 5, 6, 7):
            s16 = s16 + shifted(d, LANES, 2 * LANES)
        mean_hi = jnp.where(first, s8, s16) / cnt_ref[:, LANES:2 * LANES]
        d_lo = mean_lo - shifted(0, 0, LANES)
        d_hi = mean_hi - shifted(0, LANES, 2 * LANES)
        pd_ref[:, 0:LANES] = d_lo.astype(BF16)
        pd_ref[:, LANES:2 * LANES] = d_hi.astype(BF16)
        return _zero_bits(jnp.concatenate([d_lo, d_hi], axis=-1))

    return [halo_stage, pool_stage, shift_stage] + [conv_stage(c) for c in range(tm // CONV_CHUNK)]


def _convpool_outputs(hn_ref, pd_ref, w_pw_ref, w_pool_ref, pscale_ref):
    y_conv = jnp.dot(hn_ref[...], w_pw_ref[...], preferred_element_type=F32)
    y_pool = jnp.dot(pd_ref[...], w_pool_ref[...], preferred_element_type=F32) * pscale_ref[...]
    return y_conv.astype(BF16), y_pool.astype(BF16)


def _layer_norm(x, g, b):
    mu = jnp.mean(x, axis=-1, keepdims=True)
    xc = x - mu
    var = jnp.mean(xc * xc, axis=-1, keepdims=True)
    return xc * lax.rsqrt(var + LN_EPS) * g + b


def _mixffn_kernel(x_ref, ya_ref,
                   uc0_ref, uc0_next_ref, up0_ref, up0_next_ref, cnt0_ref,
                   uc_ref, uc_prev_ref, uc_next_ref, up_ref, up_prev_ref, up_next_ref, cnt_ref,
                   w_dw_ref, b_dw_ref, g_cn_ref, b_cn_ref, w_pw_ref, w_pool_ref, pscale_ref,
                   w_out_ref, ln1_g_ref, ln1_b_ref, w_gate_ref, w_up_ref, w_down_ref,
                   ln2_g_ref, ln2_b_ref,
                   o_ref, hs_ref, hsh_ref, ps_ref, hn_ref, pd_ref, h_ref, *, seq_len):
    tm = x_ref.shape[0]
    tiles_per_seq = seq_len // tm
    i = pl.program_id(0)
    vector_params = (w_dw_ref, b_dw_ref, g_cn_ref, b_cn_ref, hs_ref, hsh_ref, ps_ref, hn_ref, pd_ref)

    @pl.when(i == 0)
    def _():
        for stage in _convpool_stages(uc0_ref, None, uc0_next_ref, up0_ref, None, up0_next_ref,
                                      0.0, 1.0 if tiles_per_seq > 1 else 0.0, cnt0_ref,
                                      *vector_params):
            stage()

    y_conv, y_pool = _convpool_outputs(hn_ref, pd_ref, w_pw_ref, w_pool_ref, pscale_ref)
    mix = jnp.dot(y_conv, w_out_ref[0:CONV_W, :], preferred_element_type=F32)
    mix += jnp.dot(y_pool, w_out_ref[CONV_W:CONV_W + POOL_W, :], preferred_element_type=F32)
    mix += jnp.dot(ya_ref[...], w_out_ref[CONV_W + POOL_W:, :], preferred_element_type=F32)
    x1 = _layer_norm(ALPHA * x_ref[...] + mix, ln1_g_ref[...], ln1_b_ref[...])
    x1b = x1.astype(BF16)

    j = jnp.minimum(i + 1, pl.num_programs(0) - 1) % tiles_per_seq
    next_stages = _convpool_stages(
        uc_ref, uc_prev_ref, uc_next_ref, up_ref, up_prev_ref, up_next_ref,
        (j > 0).astype(F32), (j < tiles_per_seq - 1).astype(F32), cnt_ref, *vector_params)
    n_chunks = D_FF // FF_CHUNK
    assert len(next_stages) <= n_chunks
    tokens = []
    for c in range(n_chunks):
        cols = slice(c * FF_CHUNK, (c + 1) * FF_CHUNK)
        g = jnp.dot(x1b, w_gate_ref[:, cols], preferred_element_type=F32)
        u = jnp.dot(x1b, w_up_ref[:, cols], preferred_element_type=F32)
        hc = g * jax.nn.sigmoid(g) * u
        token = tokens[c - TOKEN_LAG] if c >= TOKEN_LAG else None
        h_ref[:, cols] = (hc if token is None else _add_token(hc, token)).astype(BF16)
        tokens.append(next_stages[c]() if c < len(next_stages) else None)
    y = jnp.dot(h_ref[...], w_down_ref[...], preferred_element_type=F32)
    for token in tokens[n_chunks - TOKEN_LAG:]:
        if token is not None:
            y = _add_token(y, token)
    o_ref[...] = _layer_norm(ALPHA * x1 + y, ln2_g_ref[...], ln2_b_ref[...])


def _mixffn_call(x, ya, uconv, upool, pool_cnt, seq_len, w_dw, b_dw, g_cn, b_cn, w_pw, w_pool_bd, pool_scale,
                 w_out, ln1_g, ln1_b, w_gate, w_up, w_down, ln2_g, ln2_b):
    T = x.shape[0]
    tm = TM_FFN
    n = T // tm
    r = tm // HALO
    n_halo_blocks = T // HALO

    def row(i):
        return (i, 0)

    def nxt_tile(i):
        return (jnp.minimum(i + 1, n - 1), 0)

    def nxt_tile_prev_halo(i):
        return (jnp.maximum(jnp.minimum(i + 1, n - 1) * r - 1, 0), 0)

    def nxt_tile_next_halo(i):
        return (jnp.minimum((jnp.minimum(i + 1, n - 1) + 1) * r, n_halo_blocks - 1), 0)

    def first_tile(i):
        return (0, 0)

    def first_tile_next_halo(i):
        return (r, 0)

    def nxt_tile_in_seq(i):
        return (jnp.minimum(i + 1, n - 1) % (seq_len // tm), 0)

    return pl.pallas_call(
        functools.partial(_mixffn_kernel, seq_len=seq_len),
        grid=(n,),
        in_specs=[
            pl.BlockSpec((tm, D_MODEL), row),
            pl.BlockSpec((tm, ATTN_W), row),
            pl.BlockSpec((tm, 2 * CONV_W), first_tile),
            pl.BlockSpec((HALO, 2 * CONV_W), first_tile_next_halo),
            pl.BlockSpec((tm, POOL_W), first_tile),
            pl.BlockSpec((HALO, POOL_W), first_tile_next_halo),
            pl.BlockSpec((tm, POOL_W), first_tile),
            pl.BlockSpec((tm, 2 * CONV_W), nxt_tile),
            pl.BlockSpec((HALO, 2 * CONV_W), nxt_tile_prev_halo),
            pl.BlockSpec((HALO, 2 * CONV_W), nxt_tile_next_halo),
            pl.BlockSpec((tm, POOL_W), nxt_tile),
            pl.BlockSpec((HALO, POOL_W), nxt_tile_prev_halo),
            pl.BlockSpec((HALO, POOL_W), nxt_tile_next_halo),
            pl.BlockSpec((tm, POOL_W), nxt_tile_in_seq),
            _const_spec((CONV_KERNEL, SUBLANES, CONV_W)),
            _const_spec((1, CONV_W)),
            _const_spec((1, CONV_W)),
            _const_spec((1, CONV_W)),
            _const_spec((CONV_W, CONV_W)),
            _const_spec((POOL_W, POOL_W)),
            _const_spec((1, POOL_W)),
            _const_spec((D_MODEL, D_MODEL)),
            _const_spec((1, D_MODEL)),
            _const_spec((1, D_MODEL)),
            _const_spec((D_MODEL, D_FF)),
            _const_spec((D_MODEL, D_FF)),
            _const_spec((D_FF, D_MODEL)),
            _const_spec((1, D_MODEL)),
            _const_spec((1, D_MODEL)),
        ],
        out_specs=pl.BlockSpec((tm, D_MODEL), row),
        out_shape=jax.ShapeDtypeStruct((T, D_MODEL), F32),
        scratch_shapes=[
            pltpu.VMEM((tm + 2 * HALO, CONV_W), F32),
            pltpu.VMEM((SUBLANES - 1, tm + 2 * HALO - SUBLANES, CONV_W), F32),
            pltpu.VMEM((tm + 2 * HALO, POOL_W), F32),
            pltpu.VMEM((tm, CONV_W), BF16),
            pltpu.VMEM((tm, POOL_W), BF16),
            pltpu.VMEM((tm, D_FF), BF16),
        ],
        compiler_params=pltpu.CompilerParams(
            dimension_semantics=("arbitrary",), vmem_limit_bytes=VMEM_LIMIT),
        name="mixffn",
    )(x, ya, uconv, uconv, upool, upool, pool_cnt, uconv, uconv, uconv, upool, upool, upool, pool_cnt,
      w_dw, b_dw, g_cn, b_cn, w_pw, w_pool_bd, pool_scale,
      w_out, ln1_g, ln1_b, w_gate, w_up, w_down, ln2_g, ln2_b)


def _rope_tables(S):
    pos = jnp.arange(S, dtype=F32)
    inv_freq = 1.0 / (ROPE_BASE ** (jnp.arange(0, QK_ROPE_DIM, 2, dtype=F32) / QK_ROPE_DIM))
    ang = pos[:, None] * inv_freq[None, :]
    cos, sin = jnp.cos(ang), jnp.sin(ang)
    zeros = jnp.zeros((S, LANES - QK_ROPE_DIM), F32)
    cosf = jnp.concatenate([cos, cos, zeros], axis=-1)
    sinf = jnp.concatenate([-sin, sin, zeros], axis=-1)
    return cosf, sinf, cos.T, sin.T


def _pool_counts(S):
    t = jnp.arange(S)
    cols = []
    for w in POOL_WINDOWS:
        left = w // 2
        right = w - 1 - left
        cnt = (jnp.minimum(t + right, S - 1) - jnp.maximum(t - left, 0) + 1).astype(F32)
        cols.append(jnp.broadcast_to(cnt[:, None], (S, POOL_GW)))
    return jnp.concatenate(cols, axis=-1)


def _prepare_layer(w_in, w_pw, w_pool, w_uq, w_ukv, w_out, w_gate, w_up, w_down):
    w_in_p = jnp.pad(w_in, ((0, 0), (0, D_IN_PAD - D_IN))).astype(BF16)
    wq = w_uq.reshape(Q_LORA_RANK, N_HEADS, QK_HEAD_DIM)
    w_uqT = jnp.concatenate(
        [wq[:, :, :QK_NOPE_DIM].reshape(Q_LORA_RANK, N_HEADS * QK_NOPE_DIM),
         wq[:, :, QK_NOPE_DIM:].reshape(Q_LORA_RANK, N_HEADS * QK_ROPE_DIM)], axis=-1).T.astype(BF16)
    wkv = w_ukv.reshape(KV_LORA_RANK, N_HEADS, QK_NOPE_DIM + V_HEAD_DIM)
    w_uk = wkv[:, :, :QK_NOPE_DIM].reshape(KV_LORA_RANK, N_HEADS * QK_NOPE_DIM).astype(BF16)
    w_uvT = wkv[:, :, QK_NOPE_DIM:].reshape(KV_LORA_RANK, N_HEADS * V_HEAD_DIM).T.astype(BF16)
    w_pool_bd = jnp.zeros((POOL_W, POOL_W), F32)
    for g in range(len(POOL_WINDOWS)):
        w_pool_bd = w_pool_bd.at[g * POOL_GW:(g + 1) * POOL_GW, g * POOL_GW:(g + 1) * POOL_GW].set(w_pool[g])
    return (w_in_p, w_pw.astype(BF16), w_pool_bd.astype(BF16), w_uqT, w_uk, w_uvT,
            w_out.astype(BF16), w_gate.astype(BF16), w_up.astype(BF16), w_down.astype(BF16))


def _trunk(x, layers, small):
    B, S, _ = x.shape
    cosf, sinf, cosT, sinT = _rope_tables(S)
    pool_cnt = _pool_counts(S)
    for l in range(DEPTH):
        w_in_p, w_pw_b, w_pool_bd, w_uqT, w_uk, w_uvT, w_out_b, w_gate_b, w_up_b, w_down_b = layers[l]
        (w_dw, b_dw, g_cn, b_cn, pool_scale, g_q, g_kv, ln1_g, ln1_b, ln2_g, ln2_b) = small[l]
        uconv, upool, qT, k, vT = _prep_call(
            x, w_in_p, g_q, g_kv, w_uqT, w_uk, w_uvT, cosf, sinf, cosT, sinT)
        yattn = _attn_call(qT, k, vT)
        x = _mixffn_call(
            x.reshape(B * S, D_MODEL), yattn.reshape(B * S, ATTN_W),
            uconv.reshape(B * S, 2 * CONV_W), upool.reshape(B * S, POOL_W), pool_cnt, S,
            w_dw, b_dw, g_cn, b_cn, w_pw_b, w_pool_bd, pool_scale,
            w_out_b, ln1_g, ln1_b, w_gate_b, w_up_b, w_down_b, ln2_g, ln2_b).reshape(B, S, D_MODEL)
    return x


def kernel(x_prompt, x_sample, w_in, w_dw, b_dw, g_cn, b_cn, w_pw, w_pool, pool_scale, g_q, g_kv,
           w_uq, w_ukv, w_out, ln1_g, ln1_b, w_gate, w_up, w_down, ln2_g, ln2_b):
    layers = [_prepare_layer(w_in[l], w_pw[l], w_pool[l], w_uq[l], w_ukv[l], w_out[l],
                             w_gate[l], w_up[l], w_down[l]) for l in range(DEPTH)]

    def row(a, l):
        return a[l].reshape(1, -1)

    def sublane_rows(a):
        return jnp.broadcast_to(a[:, None, :], (a.shape[0], SUBLANES, a.shape[1]))

    small = [(sublane_rows(w_dw[l]), row(b_dw, l), row(g_cn, l), row(b_cn, l), row(pool_scale, l), row(g_q, l),
              row(g_kv, l), row(ln1_g, l), row(ln1_b, l), row(ln2_g, l), row(ln2_b, l))
             for l in range(DEPTH)]
    y_prompt = _trunk(x_prompt, layers, small)
    y_sample = _trunk(x_sample, layers, small)
    return (y_prompt, y_sample)
```

```python
import functools

import jax
import jax.numpy as jnp
from jax import lax
from jax.experimental import pallas as pl
from jax.experimental.pallas import tpu as pltpu

F32 = jnp.float32
BF16 = jnp.bfloat16

D_MODEL = 1024
DEPTH = 4
CONV_W = 256
CONV_KERNEL = 31
CONV_PAD = CONV_KERNEL // 2
POOL_W = 256
POOL_WINDOWS = (2, 4, 8, 16)
POOL_GW = POOL_W // len(POOL_WINDOWS)
ATTN_W = 512
N_HEADS = 4
V_HEAD_DIM = 128
QK_NOPE_DIM = 128
QK_ROPE_DIM = 64
QK_HEAD_DIM = QK_NOPE_DIM + QK_ROPE_DIM
Q_LORA_RANK = 256
KV_LORA_RANK = 128
ROPE_BASE = 10000.0
D_IN = 2 * CONV_W + POOL_W + Q_LORA_RANK + KV_LORA_RANK + QK_ROPE_DIM
D_FF = 2816
ALPHA = (2.0 * DEPTH) ** 0.25
LN_EPS = 1e-5
RMS_EPS = 1e-6
QK_SCALE_LOG2E = QK_HEAD_DIM ** -0.5 * 1.4426950408889634

LANES = 128
SUBLANES = 8
MXU_DIM = 256
D_IN_PAD = 5 * MXU_DIM
QK_PAD = 2 * LANES
V_ROWS = V_HEAD_DIM + 16
HALO = 16
VMEM_LIMIT = 56 * 1024 * 1024

TM_PREP = 1024
PREP_SUB = 256
CONV_CHUNK = 64
TQ = 512
TK = 512
ATTN_MAX_UNROLL = 5
ATTN_TILES_PER_STEP = 8
TM_FFN = 512
FF_CHUNK = 256
TOKEN_LAG = 1


def _const_spec(shape):
    nd = len(shape)
    return pl.BlockSpec(shape, lambda *_: (0,) * nd, pipeline_mode=pl.Buffered(1))


def _swap_rope_halves(x):
    lane = lax.broadcasted_iota(jnp.int32, x.shape, 1)
    return jnp.where(lane < QK_ROPE_DIM // 2,
                     pltpu.roll(x, LANES - QK_ROPE_DIM // 2, 1),
                     pltpu.roll(x, QK_ROPE_DIM // 2, 1))


def _prep_kernel(x_ref, w_in_ref, g_q_ref, g_kv_ref, w_uqT_ref, w_uk_ref, w_uvT_ref,
                 cos_ref, sin_ref, cosT_ref, sinT_ref,
                 uconv_ref, upool_ref, qT_ref, k_ref, vT_ref):
    tm = x_ref.shape[1]
    o_q = 2 * CONV_W + POOL_W
    o_kv = o_q + Q_LORA_RANK
    o_kr = o_kv + KV_LORA_RANK
    half = QK_ROPE_DIM // 2
    rope0 = N_HEADS * QK_NOPE_DIM

    def in_proj(rows):
        xb = x_ref[0, rows, :].astype(BF16)
        return jnp.dot(xb, w_in_ref[...], preferred_element_type=F32)

    def finish(rows, u):
        sub = rows.stop - rows.start
        uconv_ref[0, rows, :] = u[:, :2 * CONV_W]
        upool_ref[0, rows, :] = u[:, 2 * CONV_W:o_q]
        c_q = u[:, o_q:o_kv]
        c_kv = u[:, o_kv:o_kr]
        kr = u[:, o_kr:o_kr + LANES]
        ms_q = jnp.mean(c_q * c_q, axis=-1, keepdims=True)
        cqn = c_q * lax.rsqrt(ms_q + RMS_EPS) * g_q_ref[...]
        ms_kv = jnp.mean(c_kv * c_kv, axis=-1, keepdims=True)
        ckvn = c_kv * lax.rsqrt(ms_kv + RMS_EPS) * g_kv_ref[...]
        qT = jnp.dot(w_uqT_ref[...], cqn.T.astype(BF16), preferred_element_type=F32)
        vT = jnp.dot(w_uvT_ref[...], ckvn.T.astype(BF16), preferred_element_type=F32)
        kn = jnp.dot(ckvn.astype(BF16), w_uk_ref[...], preferred_element_type=F32)
        cosf = cos_ref[rows, :]
        sinf = sin_ref[rows, :]
        kro = (kr * cosf + _swap_rope_halves(kr) * sinf).astype(BF16)
        cosT = cosT_ref[:, rows]
        sinT = sinT_ref[:, rows]
        chunk, off = rows.start // TK, rows.start % TK
        qt, qoff = rows.start // TQ, rows.start % TQ
        qcols = slice(qoff, qoff + sub)
        for h in range(N_HEADS):
            nope = qT[h * QK_NOPE_DIM:(h + 1) * QK_NOPE_DIM]
            x1 = qT[rope0 + h * QK_ROPE_DIM:rope0 + h * QK_ROPE_DIM + half]
            x2 = qT[rope0 + h * QK_ROPE_DIM + half:rope0 + (h + 1) * QK_ROPE_DIM]
            qT_ref[0, h, qt, 0:QK_NOPE_DIM, qcols] = (nope * QK_SCALE_LOG2E).astype(BF16)
            qT_ref[0, h, qt, QK_NOPE_DIM:QK_NOPE_DIM + half, qcols] = (
                (x1 * cosT - x2 * sinT) * QK_SCALE_LOG2E).astype(BF16)
            qT_ref[0, h, qt, QK_NOPE_DIM + half:QK_HEAD_DIM, qcols] = (
                (x1 * sinT + x2 * cosT) * QK_SCALE_LOG2E).astype(BF16)
            qT_ref[0, h, qt, QK_HEAD_DIM:QK_PAD, qcols] = jnp.zeros((QK_PAD - QK_HEAD_DIM, sub), BF16)
            k_ref[0, h, rows, 0:LANES] = kn[:, h * LANES:(h + 1) * LANES].astype(BF16)
            k_ref[0, h, rows, LANES:QK_PAD] = kro
            vT_ref[0, h, chunk, 0:V_HEAD_DIM, off:off + sub] = (
                vT[h * V_HEAD_DIM:(h + 1) * V_HEAD_DIM].astype(BF16))
            vT_ref[0, h, chunk, V_HEAD_DIM:V_ROWS, off:off + sub] = jnp.ones(
                (V_ROWS - V_HEAD_DIM, sub), BF16)

    subs = [slice(r, r + PREP_SUB) for r in range(0, tm, PREP_SUB)]
    u_prev = in_proj(subs[0])
    for n in range(1, len(subs)):
        u_next = in_proj(subs[n])
        finish(subs[n - 1], u_prev)
        u_prev = u_next
    finish(subs[-1], u_prev)


def _prep_call(x, w_in, g_q, g_kv, w_uqT, w_uk, w_uvT, cosf, sinf, cosT, sinT):
    B, S, _ = x.shape
    tm = TM_PREP
    assert tm % TK == 0 and TK % PREP_SUB == 0 and tm % TQ == 0 and TQ % PREP_SUB == 0
    grid = (B, S // tm)
    return pl.pallas_call(
        _prep_kernel,
        grid=grid,
        in_specs=[
            pl.BlockSpec((1, tm, D_MODEL), lambda b, i: (b, i, 0)),
            _const_spec((D_MODEL, D_IN_PAD)),
            _const_spec((1, Q_LORA_RANK)),
            _const_spec((1, KV_LORA_RANK)),
            _const_spec((N_HEADS * QK_HEAD_DIM, Q_LORA_RANK)),
            _const_spec((KV_LORA_RANK, N_HEADS * QK_NOPE_DIM)),
            _const_spec((N_HEADS * V_HEAD_DIM, KV_LORA_RANK)),
            pl.BlockSpec((tm, LANES), lambda b, i: (i, 0)),
            pl.BlockSpec((tm, LANES), lambda b, i: (i, 0)),
            pl.BlockSpec((QK_ROPE_DIM // 2, tm), lambda b, i: (0, i)),
            pl.BlockSpec((QK_ROPE_DIM // 2, tm), lambda b, i: (0, i)),
        ],
        out_specs=[
            pl.BlockSpec((1, tm, 2 * CONV_W), lambda b, i: (b, i, 0)),
            pl.BlockSpec((1, tm, POOL_W), lambda b, i: (b, i, 0)),
            pl.BlockSpec((1, N_HEADS, tm // TQ, QK_PAD, TQ), lambda b, i: (b, 0, i, 0, 0)),
            pl.BlockSpec((1, N_HEADS, tm, QK_PAD), lambda b, i: (b, 0, i, 0)),
            pl.BlockSpec((1, N_HEADS, tm // TK, V_ROWS, TK), lambda b, i: (b, 0, i, 0, 0)),
        ],
        out_shape=[
            jax.ShapeDtypeStruct((B, S, 2 * CONV_W), F32),
            jax.ShapeDtypeStruct((B, S, POOL_W), F32),
            jax.ShapeDtypeStruct((B, N_HEADS, S // TQ, QK_PAD, TQ), BF16),
            jax.ShapeDtypeStruct((B, N_HEADS, S, QK_PAD), BF16),
            jax.ShapeDtypeStruct((B, N_HEADS, S // TK, V_ROWS, TK), BF16),
        ],
        compiler_params=pltpu.CompilerParams(
            dimension_semantics=("arbitrary", "arbitrary"), vmem_limit_bytes=VMEM_LIMIT),
        name="prep",
    )(x, w_in, g_q, g_kv, w_uqT, w_uk, w_uvT, cosf, sinf, cosT, sinT)


def _attn_kernel(qT_ref, k_ref, vT_ref, o_ref, sa_ref, sb_ref, acc_ref, *, seq_len):
    n_tiles, _, tq = qT_ref.shape[2:]
    nk = seq_len // TK

    def scores(j, q_t, s_ref):
        start = pl.multiple_of(j * TK, TK)
        s = jnp.dot(k_ref[0, 0, pl.ds(start, TK), :], q_t, preferred_element_type=F32)
        s_ref[:, 0:tq] = s
        return jnp.max(s, axis=0, keepdims=True)

    def update(j, s_ref, cmax, m):
        m_new = jnp.maximum(m, cmax)
        alpha = jnp.exp2(m - m_new)
        p = jnp.exp2(s_ref[:, 0:tq] - m_new)
        pv = jnp.dot(vT_ref[0, 0, j], p.astype(BF16), preferred_element_type=F32)
        acc_ref[...] = acc_ref[...] * alpha + pv
        return m_new

    def tile(g, cmax_a):
        qT = qT_ref[0, 0, g]
        acc_ref[...] = jnp.zeros_like(acc_ref)
        m = jnp.full((1, tq), -jnp.inf, F32)

        def body(i, carry):
            m, cmax_a = carry
            cmax_b = scores(2 * i + 1, qT, sb_ref)
            m = update(2 * i, sa_ref, cmax_a, m)
            cmax_a = scores(2 * i + 2, qT, sa_ref)
            m = update(2 * i + 1, sb_ref, cmax_b, m)
            return m, cmax_a

        trips = nk // 2 - 1
        unroll = max(u for u in range(1, ATTN_MAX_UNROLL + 1) if trips % u == 0)
        m, cmax_a = lax.fori_loop(0, trips, body, (m, cmax_a), unroll=unroll)
        cmax_b = scores(nk - 1, qT, sb_ref)
        m = update(nk - 2, sa_ref, cmax_a, m)
        cmax_next = scores(0, qT_ref[0, 0, jnp.minimum(g + 1, n_tiles - 1)], sa_ref)
        m = update(nk - 1, sb_ref, cmax_b, m)
        l = acc_ref[V_HEAD_DIM:V_HEAD_DIM + 1, :]
        rows = pl.ds(pl.multiple_of(g * tq, tq), tq)
        o_ref[0, rows, :] = (acc_ref[0:V_HEAD_DIM, :] / l).T.astype(BF16)
        return cmax_next

    lax.fori_loop(0, n_tiles, tile, scores(0, qT_ref[0, 0, 0], sa_ref))


def _attn_call(qT, k, vT):
    B, H, S, _ = k.shape
    tq = TQ
    tiles = min(S // tq, ATTN_TILES_PER_STEP)
    assert (S // TK) % 2 == 0 and (S // tq) % tiles == 0
    return pl.pallas_call(
        functools.partial(_attn_kernel, seq_len=S),
        grid=(B, H, S // (tq * tiles)),
        in_specs=[
            pl.BlockSpec((1, 1, tiles, QK_PAD, tq), lambda b, h, i: (b, h, i, 0, 0)),
            pl.BlockSpec((1, 1, S, QK_PAD), lambda b, h, i: (b, h, 0, 0)),
            pl.BlockSpec((1, 1, S // TK, V_ROWS, TK), lambda b, h, i: (b, h, 0, 0, 0)),
        ],
        out_specs=pl.BlockSpec((1, tiles * tq, V_HEAD_DIM), lambda b, h, i: (b, i, h)),
        out_shape=jax.ShapeDtypeStruct((B, S, ATTN_W), BF16),
        scratch_shapes=[
            pltpu.VMEM((TK, tq + LANES), F32),
            pltpu.VMEM((TK, tq + LANES), F32),
            pltpu.VMEM((V_ROWS, tq), F32),
        ],
        compiler_params=pltpu.CompilerParams(
            dimension_semantics=("arbitrary", "arbitrary", "arbitrary"),
            vmem_limit_bytes=VMEM_LIMIT),
        name="attn",
    )(qT, k, vT)


def _glu(u):
    return u[:, :CONV_W] * jax.nn.sigmoid(u[:, CONV_W:])


def _zero_bits(x):
    bits = pltpu.bitcast(x, jnp.uint32)
    z = lax.shift_right_logical(lax.shift_right_logical(bits, jnp.uint32(16)), jnp.uint32(16))
    z = functools.reduce(
        jnp.bitwise_or, [z[r:r + SUBLANES] for r in range(0, x.shape[0], SUBLANES)])
    return pltpu.bitcast(z, jnp.int32).astype(F32)


def _add_token(x, token):
    r, c = token.shape
    top = x[0:r, 0:c] + token
    top = jnp.concatenate([top, x[0:r, c:]], axis=1) if c < x.shape[1] else top
    return jnp.concatenate([top, x[r:]], axis=0)


def _convpool_stages(uc_ref, uc_prev_ref, uc_next_ref, up_ref, up_prev_ref, up_next_ref,
                     has_prev, has_next, cnt_ref,
                     w_dw_ref, b_dw_ref, g_cn_ref, b_cn_ref,
                     hs_ref, hsh_ref, ps_ref, hn_ref, pd_ref):
    tm = uc_ref.shape[0]
    shift_rows = hsh_ref.shape[1]

    def halo_stage():
        if uc_prev_ref is None:
            hs_ref[0:HALO, :] = jnp.zeros((HALO, CONV_W), F32)
            ps_ref[0:HALO, :] = jnp.zeros((HALO, POOL_W), F32)
        else:
            hs_ref[0:HALO, :] = _glu(uc_prev_ref[...]) * has_prev
            ps_ref[0:HALO, :] = up_prev_ref[...] * has_prev
        hs_ref[HALO:HALO + tm, :] = _glu(uc_ref[...])
        hs_ref[HALO + tm:2 * HALO + tm, :] = _glu(uc_next_ref[...]) * has_next
        ps_ref[HALO:HALO + tm, :] = up_ref[...]
        ps_ref[HALO + tm:2 * HALO + tm, :] = up_next_ref[...] * has_next

    def shift_stage():
        for s in range(1, SUBLANES):
            hsh_ref[s - 1, 0:shift_rows, :] = hs_ref[s:s + shift_rows, :]

    def tap(c, k):
        off = HALO - CONV_PAD + k
        s, row = off % SUBLANES, c * CONV_CHUNK + off - off % SUBLANES
        if s == 0:
            return hs_ref[row:row + CONV_CHUNK, :]
        return hsh_ref[s - 1, row:row + CONV_CHUNK, :]

    def conv_stage(c):
        def run():
            groups = (CONV_CHUNK // SUBLANES, SUBLANES, CONV_W)
            acc = jnp.broadcast_to(b_dw_ref[...], groups)
            for k in range(CONV_KERNEL):
                acc = acc + tap(c, k).reshape(groups) * w_dw_ref[k]
            acc = acc.reshape(CONV_CHUNK, CONV_W)
            mu = jnp.mean(acc, axis=-1, keepdims=True)
            xc = acc - mu
            var = jnp.mean(xc * xc, axis=-1, keepdims=True)
            hn = xc * lax.rsqrt(var + LN_EPS) * g_cn_ref[...] + b_cn_ref[...]
            act = hn * jax.nn.sigmoid(hn)
            hn_ref[c * CONV_CHUNK:(c + 1) * CONV_CHUNK, :] = act.astype(BF16)
            return _zero_bits(act)
        return run

    def pool_stage():
        def shifted(d, lo, hi):
            return ps_ref[HALO + d:HALO + d + tm, lo:hi]

        lane = lax.broadcasted_iota(jnp.int32, (tm, LANES), 1)
        first = lane < POOL_GW
        s2 = shifted(-1, 0, LANES) + shifted(0, 0, LANES)
        s4 = s2 + shifted(-2, 0, LANES) + shifted(1, 0, LANES)
        mean_lo = jnp.where(first, s2, s4) / cnt_ref[:, 0:LANES]
        s8 = shifted(-4, LANES, 2 * LANES)
        for d in range(-3, 4):
            s8 = s8 + shifted(d, LANES, 2 * LANES)
        s16 = s8
        for d in (-8, -7, -6, -5, 4, 5, 6, 7):
            s16 = s16 + shifted(d, LANES, 2 * LANES)
        mean_hi = jnp.where(first, s8, s16) / cnt_ref[:, LANES:2 * LANES]
        d_lo = mean_lo - shifted(0, 0, LANES)
        d_hi = mean_hi - shifted(0, LANES, 2 * LANES)
        pd_ref[:, 0:LANES] = d_lo.astype(BF16)
        pd_ref[:, LANES:2 * LANES] = d_hi.astype(BF16)
        return _zero_bits(jnp.concatenate([d_lo, d_hi], axis=-1))

    return [halo_stage, pool_stage, shift_stage] + [conv_stage(c) for c in range(tm // CONV_CHUNK)]


def _convpool_outputs(hn_ref, pd_ref, w_pw_ref, w_pool_ref, pscale_ref):
    y_conv = jnp.dot(hn_ref[...], w_pw_ref[...], preferred_element_type=F32)
    y_pool = jnp.dot(pd_ref[...], w_pool_ref[...], preferred_element_type=F32) * pscale_ref[...]
    return y_conv.astype(BF16), y_pool.astype(BF16)


def _layer_norm(x, g, b):
    mu = jnp.mean(x, axis=-1, keepdims=True)
    xc = x - mu
    var = jnp.mean(xc * xc, axis=-1, keepdims=True)
    return xc * lax.rsqrt(var + LN_EPS) * g + b


def _mixffn_kernel(x0_ref, ya0_ref, x_next_ref, ya_next_ref,
                   uc0_ref, uc0_next_ref, up0_ref, up0_next_ref, cnt0_ref,
                   uc_ref, uc_prev_ref, uc_next_ref, up_ref, up_prev_ref, up_next_ref, cnt_ref,
                   w_dw_ref, b_dw_ref, g_cn_ref, b_cn_ref, w_pw_ref, w_pool_ref, pscale_ref,
                   w_out_ref, ln1_g_ref, ln1_b_ref, w_gate_ref, w_up_ref, w_down_ref,
                   ln2_g_ref, ln2_b_ref,
                   o_ref, hs_ref, hsh_ref, ps_ref, hn_ref, pd_ref, x1_ref, h_ref, *, seq_len):
    tm = o_ref.shape[0]
    tiles_per_seq = seq_len // tm
    i = pl.program_id(0)
    slot = i % 2
    vector_params = (w_dw_ref, b_dw_ref, g_cn_ref, b_cn_ref, hs_ref, hsh_ref, ps_ref, hn_ref, pd_ref)

    def mixed_norm(xr_ref, yar_ref):
        y_conv, y_pool = _convpool_outputs(hn_ref, pd_ref, w_pw_ref, w_pool_ref, pscale_ref)
        mix = jnp.dot(y_conv, w_out_ref[0:CONV_W, :], preferred_element_type=F32)
        mix += jnp.dot(y_pool, w_out_ref[CONV_W:CONV_W + POOL_W, :], preferred_element_type=F32)
        mix += jnp.dot(yar_ref[...], w_out_ref[CONV_W + POOL_W:, :], preferred_element_type=F32)
        return _layer_norm(ALPHA * xr_ref[...] + mix, ln1_g_ref[...], ln1_b_ref[...])

    @pl.when(i == 0)
    def _():
        for stage in _convpool_stages(uc0_ref, None, uc0_next_ref, up0_ref, None, up0_next_ref,
                                      0.0, 1.0 if tiles_per_seq > 1 else 0.0, cnt0_ref,
                                      *vector_params):
            stage()
        x1_ref[0] = mixed_norm(x0_ref, ya0_ref)

    x1b = x1_ref[slot].astype(BF16)

    j = jnp.minimum(i + 1, pl.num_programs(0) - 1) % tiles_per_seq
    next_stages = _convpool_stages(
        uc_ref, uc_prev_ref, uc_next_ref, up_ref, up_prev_ref, up_next_ref,
        (j > 0).astype(F32), (j < tiles_per_seq - 1).astype(F32), cnt_ref, *vector_params)
    n_chunks = D_FF // FF_CHUNK
    assert len(next_stages) <= n_chunks
    tokens = []
    for c in range(n_chunks):
        cols = slice(c * FF_CHUNK, (c + 1) * FF_CHUNK)
        g = jnp.dot(x1b, w_gate_ref[:, cols], preferred_element_type=F32)
        u = jnp.dot(x1b, w_up_ref[:, cols], preferred_element_type=F32)
        hc = g * jax.nn.sigmoid(g) * u
        token = tokens[c - TOKEN_LAG] if c >= TOKEN_LAG else None
        h_ref[:, cols] = (hc if token is None else _add_token(hc, token)).astype(BF16)
        tokens.append(next_stages[c]() if c < len(next_stages) else None)
    x1_ref[1 - slot] = mixed_norm(x_next_ref, ya_next_ref)
    y = jnp.dot(h_ref[...], w_down_ref[...], preferred_element_type=F32)
    o_ref[...] = _layer_norm(ALPHA * x1_ref[slot] + y, ln2_g_ref[...], ln2_b_ref[...])


def _mixffn_call(x, ya, uconv, upool, pool_cnt, seq_len, w_dw, b_dw, g_cn, b_cn, w_pw, w_pool_bd, pool_scale,
                 w_out, ln1_g, ln1_b, w_gate, w_up, w_down, ln2_g, ln2_b):
    T = x.shape[0]
    tm = TM_FFN
    n = T // tm
    r = tm // HALO
    n_halo_blocks = T // HALO

    def row(i):
        return (i, 0)

    def nxt_tile(i):
        return (jnp.minimum(i + 1, n - 1), 0)

    def nxt_tile_prev_halo(i):
        return (jnp.maximum(jnp.minimum(i + 1, n - 1) * r - 1, 0), 0)

    def nxt_tile_next_halo(i):
        return (jnp.minimum((jnp.minimum(i + 1, n - 1) + 1) * r, n_halo_blocks - 1), 0)

    once = pl.Buffered(1)

    def first_tile(i):
        return (0, 0)

    def first_tile_next_halo(i):
        return (r, 0)

    def nxt_tile_in_seq(i):
        return (jnp.minimum(i + 1, n - 1) % (seq_len // tm), 0)

    return pl.pallas_call(
        functools.partial(_mixffn_kernel, seq_len=seq_len),
        grid=(n,),
        in_specs=[
            pl.BlockSpec((tm, D_MODEL), first_tile, pipeline_mode=once),
            pl.BlockSpec((tm, ATTN_W), first_tile, pipeline_mode=once),
            pl.BlockSpec((tm, D_MODEL), nxt_tile),
            pl.BlockSpec((tm, ATTN_W), nxt_tile),
            pl.BlockSpec((tm, 2 * CONV_W), first_tile, pipeline_mode=once),
            pl.BlockSpec((HALO, 2 * CONV_W), first_tile_next_halo, pipeline_mode=once),
            pl.BlockSpec((tm, POOL_W), first_tile, pipeline_mode=once),
            pl.BlockSpec((HALO, POOL_W), first_tile_next_halo, pipeline_mode=once),
            pl.BlockSpec((tm, POOL_W), first_tile, pipeline_mode=once),
            pl.BlockSpec((tm, 2 * CONV_W), nxt_tile),
            pl.BlockSpec((HALO, 2 * CONV_W), nxt_tile_prev_halo),
            pl.BlockSpec((HALO, 2 * CONV_W), nxt_tile_next_halo),
            pl.BlockSpec((tm, POOL_W), nxt_tile),
            pl.BlockSpec((HALO, POOL_W), nxt_tile_prev_halo),
            pl.BlockSpec((HALO, POOL_W), nxt_tile_next_halo),
            pl.BlockSpec((tm, POOL_W), nxt_tile_in_seq),
            _const_spec((CONV_KERNEL, SUBLANES, CONV_W)),
            _const_spec((1, CONV_W)),
            _const_spec((1, CONV_W)),
            _const_spec((1, CONV_W)),
            _const_spec((CONV_W, CONV_W)),
            _const_spec((POOL_W, POOL_W)),
            _const_spec((1, POOL_W)),
            _const_spec((D_MODEL, D_MODEL)),
            _const_spec((1, D_MODEL)),
            _const_spec((1, D_MODEL)),
            _const_spec((D_MODEL, D_FF)),
            _const_spec((D_MODEL, D_FF)),
            _const_spec((D_FF, D_MODEL)),
            _const_spec((1, D_MODEL)),
            _const_spec((1, D_MODEL)),
        ],
        out_specs=pl.BlockSpec((tm, D_MODEL), row),
        out_shape=jax.ShapeDtypeStruct((T, D_MODEL), F32),
        scratch_shapes=[
            pltpu.VMEM((tm + 2 * HALO, CONV_W), F32),
            pltpu.VMEM((SUBLANES - 1, tm + 2 * HALO - SUBLANES, CONV_W), F32),
            pltpu.VMEM((tm + 2 * HALO, POOL_W), F32),
            pltpu.VMEM((tm, CONV_W), BF16),
            pltpu.VMEM((tm, POOL_W), BF16),
            pltpu.VMEM((2, tm, D_MODEL), F32),
            pltpu.VMEM((tm, D_FF), BF16),
        ],
        compiler_params=pltpu.CompilerParams(
            dimension_semantics=("arbitrary",), vmem_limit_bytes=VMEM_LIMIT),
        name="mixffn",
    )(x, ya, x, ya, uconv, uconv, upool, upool, pool_cnt, uconv, uconv, uconv, upool, upool, upool, pool_cnt,
      w_dw, b_dw, g_cn, b_cn, w_pw, w_pool_bd, pool_scale,
      w_out, ln1_g, ln1_b, w_gate, w_up, w_down, ln2_g, ln2_b)


def _rope_tables(S):
    pos = jnp.arange(S, dtype=F32)
    inv_freq = 1.0 / (ROPE_BASE ** (jnp.arange(0, QK_ROPE_DIM, 2, dtype=F32) / QK_ROPE_DIM))
    ang = pos[:, None] * inv_freq[None, :]
    cos, sin = jnp.cos(ang), jnp.sin(ang)
    zeros = jnp.zeros((S, LANES - QK_ROPE_DIM), F32)
    cosf = jnp.concatenate([cos, cos, zeros], axis=-1)
    sinf = jnp.concatenate([-sin, sin, zeros], axis=-1)
    return cosf, sinf, cos.T, sin.T


def _pool_counts(S):
    t = jnp.arange(S)
    cols = []
    for w in POOL_WINDOWS:
        left = w // 2
        right = w - 1 - left
        cnt = (jnp.minimum(t + right, S - 1) - jnp.maximum(t - left, 0) + 1).astype(F32)
        cols.append(jnp.broadcast_to(cnt[:, None], (S, POOL_GW)))
    return jnp.concatenate(cols, axis=-1)


def _prepare_layer(w_in, w_pw, w_pool, w_uq, w_ukv, w_out, w_gate, w_up, w_down):
    w_in_p = jnp.pad(w_in, ((0, 0), (0, D_IN_PAD - D_IN))).astype(BF16)
    wq = w_uq.reshape(Q_LORA_RANK, N_HEADS, QK_HEAD_DIM)
    w_uqT = jnp.concatenate(
        [wq[:, :, :QK_NOPE_DIM].reshape(Q_LORA_RANK, N_HEADS * QK_NOPE_DIM),
         wq[:, :, QK_NOPE_DIM:].reshape(Q_LORA_RANK, N_HEADS * QK_ROPE_DIM)], axis=-1).T.astype(BF16)
    wkv = w_ukv.reshape(KV_LORA_RANK, N_HEADS, QK_NOPE_DIM + V_HEAD_DIM)
    w_uk = wkv[:, :, :QK_NOPE_DIM].reshape(KV_LORA_RANK, N_HEADS * QK_NOPE_DIM).astype(BF16)
    w_uvT = wkv[:, :, QK_NOPE_DIM:].reshape(KV_LORA_RANK, N_HEADS * V_HEAD_DIM).T.astype(BF16)
    w_pool_bd = jnp.zeros((POOL_W, POOL_W), F32)
    for g in range(len(POOL_WINDOWS)):
        w_pool_bd = w_pool_bd.at[g * POOL_GW:(g + 1) * POOL_GW, g * POOL_GW:(g + 1) * POOL_GW].set(w_pool[g])
    return (w_in_p, w_pw.astype(BF16), w_pool_bd.astype(BF16), w_uqT, w_uk, w_uvT,
            w_out.astype(BF16), w_gate.astype(BF16), w_up.astype(BF16), w_down.astype(BF16))


def _trunk(x, layers, small):
    B, S, _ = x.shape
    cosf, sinf, cosT, sinT = _rope_tables(S)
    pool_cnt = _pool_counts(S)
    for l in range(DEPTH):
        w_in_p, w_pw_b, w_pool_bd, w_uqT, w_uk, w_uvT, w_out_b, w_gate_b, w_up_b, w_down_b = layers[l]
        (w_dw, b_dw, g_cn, b_cn, pool_scale, g_q, g_kv, ln1_g, ln1_b, ln2_g, ln2_b) = small[l]
        uconv, upool, qT, k, vT = _prep_call(
            x, w_in_p, g_q, g_kv, w_uqT, w_uk, w_uvT, cosf, sinf, cosT, sinT)
        yattn = _attn_call(qT, k, vT)
        x = _mixffn_call(
            x.reshape(B * S, D_MODEL), yattn.reshape(B * S, ATTN_W),
            uconv.reshape(B * S, 2 * CONV_W), upool.reshape(B * S, POOL_W), pool_cnt, S,
            w_dw, b_dw, g_cn, b_cn, w_pw_b, w_pool_bd, pool_scale,
            w_out_b, ln1_g, ln1_b, w_gate_b, w_up_b, w_down_b, ln2_g, ln2_b).reshape(B, S, D_MODEL)
    return x


def kernel(x_prompt, x_sample, w_in, w_dw, b_dw, g_cn, b_cn, w_pw, w_pool, pool_scale, g_q, g_kv,
           w_uq, w_ukv, w_out, ln1_g, ln1_b, w_gate, w_up, w_down, ln2_g, ln2_b):
    layers = [_prepare_layer(w_in[l], w_pw[l], w_pool[l], w_uq[l], w_ukv[l], w_out[l],
                             w_gate[l], w_up[l], w_down[l]) for l in range(DEPTH)]

    def row(a, l):
        return a[l].reshape(1, -1)

    def sublane_rows(a):
        return jnp.broadcast_to(a[:, None, :], (a.shape[0], SUBLANES, a.shape[1]))

    small = [(sublane_rows(w_dw[l]), row(b_dw, l), row(g_cn, l), row(b_cn, l), row(pool_scale, l), row(g_q, l),
              row(g_kv, l), row(ln1_g, l), row(ln1_b, l), row(ln2_g, l), row(ln2_b, l))
             for l in range(DEPTH)]
    y_prompt = _trunk(x_prompt, layers, small)
    y_sample = _trunk(x_sample, layers, small)
    return (y_prompt, y_sample)
```

```python
import functools

import jax
import jax.numpy as jnp
from jax import lax
from jax.experimental import pallas as pl
from jax.experimental.pallas import tpu as pltpu

F32 = jnp.float32
BF16 = jnp.bfloat16

D_MODEL = 1024
DEPTH = 4
CONV_W = 256
CONV_KERNEL = 31
CONV_PAD = CONV_KERNEL // 2
POOL_W = 256
POOL_WINDOWS = (2, 4, 8, 16)
POOL_GW = POOL_W // len(POOL_WINDOWS)
ATTN_W = 512
N_HEADS = 4
V_HEAD_DIM = 128
QK_NOPE_DIM = 128
QK_ROPE_DIM = 64
QK_HEAD_DIM = QK_NOPE_DIM + QK_ROPE_DIM
Q_LORA_RANK = 256
KV_LORA_RANK = 128
ROPE_BASE = 10000.0
D_IN = 2 * CONV_W + POOL_W + Q_LORA_RANK + KV_LORA_RANK + QK_ROPE_DIM
D_FF = 2816
ALPHA = (2.0 * DEPTH) ** 0.25
LN_EPS = 1e-5
RMS_EPS = 1e-6
QK_SCALE_LOG2E = QK_HEAD_DIM ** -0.5 * 1.4426950408889634

LANES = 128
SUBLANES = 8
MXU_DIM = 256
D_IN_PAD = 5 * MXU_DIM
QK_PAD = 2 * LANES
V_ROWS = V_HEAD_DIM + 16
HALO = 16
VMEM_LIMIT = 56 * 1024 * 1024

TM_PREP = 1024
PREP_SUB = 256
CONV_CHUNK = 64
TQ = 512
TK = 512
ATTN_MAX_UNROLL = 5
ATTN_TILES_PER_STEP = 8
TM_FFN = 512
FF_CHUNK = 256
TOKEN_LAG = 1


def _const_spec(shape):
    nd = len(shape)
    return pl.BlockSpec(shape, lambda *_: (0,) * nd, pipeline_mode=pl.Buffered(1))


def _swap_rope_halves(x):
    lane = lax.broadcasted_iota(jnp.int32, x.shape, 1)
    return jnp.where(lane < QK_ROPE_DIM // 2,
                     pltpu.roll(x, LANES - QK_ROPE_DIM // 2, 1),
                     pltpu.roll(x, QK_ROPE_DIM // 2, 1))


def _prep_kernel(x_ref, w_in_ref, g_q_ref, g_kv_ref, w_uqT_ref, w_uk_ref, w_uvT_ref,
                 cos_ref, sin_ref, cosT_ref, sinT_ref,
                 uconv_ref, upool_ref, qT_ref, k_ref, vT_ref):
    tm = x_ref.shape[1]
    o_q = 2 * CONV_W + POOL_W
    o_kv = o_q + Q_LORA_RANK
    o_kr = o_kv + KV_LORA_RANK
    half = QK_ROPE_DIM // 2
    rope0 = N_HEADS * QK_NOPE_DIM

    def in_proj(rows):
        xb = x_ref[0, rows, :].astype(BF16)
        return jnp.dot(xb, w_in_ref[...], preferred_element_type=F32)

    def finish(rows, u):
        sub = rows.stop - rows.start
        uconv_ref[0, rows, :] = u[:, :2 * CONV_W]
        upool_ref[0, rows, :] = u[:, 2 * CONV_W:o_q]
        c_q = u[:, o_q:o_kv]
        c_kv = u[:, o_kv:o_kr]
        kr = u[:, o_kr:o_kr + LANES]
        ms_q = jnp.mean(c_q * c_q, axis=-1, keepdims=True)
        cqn = c_q * lax.rsqrt(ms_q + RMS_EPS) * g_q_ref[...]
        ms_kv = jnp.mean(c_kv * c_kv, axis=-1, keepdims=True)
        ckvn = c_kv * lax.rsqrt(ms_kv + RMS_EPS) * g_kv_ref[...]
        qT = jnp.dot(w_uqT_ref[...], cqn.T.astype(BF16), preferred_element_type=F32)
        vT = jnp.dot(w_uvT_ref[...], ckvn.T.astype(BF16), preferred_element_type=F32)
        kn = jnp.dot(ckvn.astype(BF16), w_uk_ref[...], preferred_element_type=F32)
        cosf = cos_ref[rows, :]
        sinf = sin_ref[rows, :]
        kro = (kr * cosf + _swap_rope_halves(kr) * sinf).astype(BF16)
        cosT = cosT_ref[:, rows]
        sinT = sinT_ref[:, rows]
        chunk, off = rows.start // TK, rows.start % TK
        qt, qoff = rows.start // TQ, rows.start % TQ
        qcols = slice(qoff, qoff + sub)
        for h in range(N_HEADS):
            nope = qT[h * QK_NOPE_DIM:(h + 1) * QK_NOPE_DIM]
            x1 = qT[rope0 + h * QK_ROPE_DIM:rope0 + h * QK_ROPE_DIM + half]
            x2 = qT[rope0 + h * QK_ROPE_DIM + half:rope0 + (h + 1) * QK_ROPE_DIM]
            qT_ref[0, h, qt, 0:QK_NOPE_DIM, qcols] = (nope * QK_SCALE_LOG2E).astype(BF16)
            qT_ref[0, h, qt, QK_NOPE_DIM:QK_NOPE_DIM + half, qcols] = (
                (x1 * cosT - x2 * sinT) * QK_SCALE_LOG2E).astype(BF16)
            qT_ref[0, h, qt, QK_NOPE_DIM + half:QK_HEAD_DIM, qcols] = (
                (x1 * sinT + x2 * cosT) * QK_SCALE_LOG2E).astype(BF16)
            qT_ref[0, h, qt, QK_HEAD_DIM:QK_PAD, qcols] = jnp.zeros((QK_PAD - QK_HEAD_DIM, sub), BF16)
            k_ref[0, h, rows, 0:LANES] = kn[:, h * LANES:(h + 1) * LANES].astype(BF16)
            k_ref[0, h, rows, LANES:QK_PAD] = kro
            vT_ref[0, h, chunk, 0:V_HEAD_DIM, off:off + sub] = (
                vT[h * V_HEAD_DIM:(h + 1) * V_HEAD_DIM].astype(BF16))
            vT_ref[0, h, chunk, V_HEAD_DIM:V_ROWS, off:off + sub] = jnp.ones(
                (V_ROWS - V_HEAD_DIM, sub), BF16)

    subs = [slice(r, r + PREP_SUB) for r in range(0, tm, PREP_SUB)]
    u_prev = in_proj(subs[0])
    for n in range(1, len(subs)):
        u_next = in_proj(subs[n])
        finish(subs[n - 1], u_prev)
        u_prev = u_next
    finish(subs[-1], u_prev)


def _prep_call(x, w_in, g_q, g_kv, w_uqT, w_uk, w_uvT, cosf, sinf, cosT, sinT):
    B, S, _ = x.shape
    tm = TM_PREP
    assert tm % TK == 0 and TK % PREP_SUB == 0 and tm % TQ == 0 and TQ % PREP_SUB == 0
    grid = (B, S // tm)
    return pl.pallas_call(
        _prep_kernel,
        grid=grid,
        in_specs=[
            pl.BlockSpec((1, tm, D_MODEL), lambda b, i: (b, i, 0)),
            _const_spec((D_MODEL, D_IN_PAD)),
            _const_spec((1, Q_LORA_RANK)),
            _const_spec((1, KV_LORA_RANK)),
            _const_spec((N_HEADS * QK_HEAD_DIM, Q_LORA_RANK)),
            _const_spec((KV_LORA_RANK, N_HEADS * QK_NOPE_DIM)),
            _const_spec((N_HEADS * V_HEAD_DIM, KV_LORA_RANK)),
            pl.BlockSpec((tm, LANES), lambda b, i: (i, 0)),
            pl.BlockSpec((tm, LANES), lambda b, i: (i, 0)),
            pl.BlockSpec((QK_ROPE_DIM // 2, tm), lambda b, i: (0, i)),
            pl.BlockSpec((QK_ROPE_DIM // 2, tm), lambda b, i: (0, i)),
        ],
        out_specs=[
            pl.BlockSpec((1, tm, 2 * CONV_W), lambda b, i: (b, i, 0)),
            pl.BlockSpec((1, tm, POOL_W), lambda b, i: (b, i, 0)),
            pl.BlockSpec((1, N_HEADS, tm // TQ, QK_PAD, TQ), lambda b, i: (b, 0, i, 0, 0)),
            pl.BlockSpec((1, N_HEADS, tm, QK_PAD), lambda b, i: (b, 0, i, 0)),
            pl.BlockSpec((1, N_HEADS, tm // TK, V_ROWS, TK), lambda b, i: (b, 0, i, 0, 0)),
        ],
        out_shape=[
            jax.ShapeDtypeStruct((B, S, 2 * CONV_W), F32),
            jax.ShapeDtypeStruct((B, S, POOL_W), F32),
            jax.ShapeDtypeStruct((B, N_HEADS, S // TQ, QK_PAD, TQ), BF16),
            jax.ShapeDtypeStruct((B, N_HEADS, S, QK_PAD), BF16),
            jax.ShapeDtypeStruct((B, N_HEADS, S // TK, V_ROWS, TK), BF16),
        ],
        compiler_params=pltpu.CompilerParams(
            dimension_semantics=("arbitrary", "arbitrary"), vmem_limit_bytes=VMEM_LIMIT),
        name="prep",
    )(x, w_in, g_q, g_kv, w_uqT, w_uk, w_uvT, cosf, sinf, cosT, sinT)


def _attn_kernel(qT_ref, k_ref, vT_ref, o_ref, sa_ref, sb_ref, acc_ref, *, seq_len):
    n_tiles, _, tq = qT_ref.shape[2:]
    nk = seq_len // TK

    def scores(j, q_t, s_ref):
        start = pl.multiple_of(j * TK, TK)
        s = jnp.dot(k_ref[0, 0, pl.ds(start, TK), :], q_t, preferred_element_type=F32)
        s_ref[:, 0:tq] = s
        return jnp.max(s, axis=0, keepdims=True)

    def update(j, s_ref, cmax, m):
        m_new = jnp.maximum(m, cmax)
        alpha = jnp.exp2(m - m_new)
        for lo in range(0, tq, MXU_DIM):
            cols = slice(lo, lo + MXU_DIM)
            p = jnp.exp2(s_ref[:, cols] - m_new[:, cols])
            pv = jnp.dot(vT_ref[0, 0, j], p.astype(BF16), preferred_element_type=F32)
            acc_ref[:, cols] = acc_ref[:, cols] * alpha[:, cols] + pv
        return m_new

    def tile(g, cmax_a):
        qT = qT_ref[0, 0, g]
        acc_ref[...] = jnp.zeros_like(acc_ref)
        m = jnp.full((1, tq), -jnp.inf, F32)

        def body(i, carry):
            m, cmax_a = carry
            cmax_b = scores(2 * i + 1, qT, sb_ref)
            m = update(2 * i, sa_ref, cmax_a, m)
            cmax_a = scores(2 * i + 2, qT, sa_ref)
            m = update(2 * i + 1, sb_ref, cmax_b, m)
            return m, cmax_a

        trips = nk // 2 - 1
        unroll = max(u for u in range(1, ATTN_MAX_UNROLL + 1) if trips % u == 0)
        m, cmax_a = lax.fori_loop(0, trips, body, (m, cmax_a), unroll=unroll)
        cmax_b = scores(nk - 1, qT, sb_ref)
        m = update(nk - 2, sa_ref, cmax_a, m)
        cmax_next = scores(0, qT_ref[0, 0, jnp.minimum(g + 1, n_tiles - 1)], sa_ref)
        m = update(nk - 1, sb_ref, cmax_b, m)
        l = acc_ref[V_HEAD_DIM:V_HEAD_DIM + 1, :]
        rows = pl.ds(pl.multiple_of(g * tq, tq), tq)
        o_ref[0, rows, :] = (acc_ref[0:V_HEAD_DIM, :] / l).T.astype(BF16)
        return cmax_next

    lax.fori_loop(0, n_tiles, tile, scores(0, qT_ref[0, 0, 0], sa_ref))


def _attn_call(qT, k, vT):
    B, H, S, _ = k.shape
    tq = TQ
    tiles = min(S // tq, ATTN_TILES_PER_STEP)
    assert (S // TK) % 2 == 0 and (S // tq) % tiles == 0
    return pl.pallas_call(
        functools.partial(_attn_kernel, seq_len=S),
        grid=(B, H, S // (tq * tiles)),
        in_specs=[
            pl.BlockSpec((1, 1, tiles, QK_PAD, tq), lambda b, h, i: (b, h, i, 0, 0)),
            pl.BlockSpec((1, 1, S, QK_PAD), lambda b, h, i: (b, h, 0, 0)),
            pl.BlockSpec((1, 1, S // TK, V_ROWS, TK), lambda b, h, i: (b, h, 0, 0, 0)),
        ],
        out_specs=pl.BlockSpec((1, tiles * tq, V_HEAD_DIM), lambda b, h, i: (b, i, h)),
        out_shape=jax.ShapeDtypeStruct((B, S, ATTN_W), BF16),
        scratch_shapes=[
            pltpu.VMEM((TK, tq + LANES), F32),
            pltpu.VMEM((TK, tq + LANES), F32),
            pltpu.VMEM((V_ROWS, tq), F32),
        ],
        compiler_params=pltpu.CompilerParams(
            dimension_semantics=("arbitrary", "arbitrary", "arbitrary"),
            vmem_limit_bytes=VMEM_LIMIT),
        name="attn",
    )(qT, k, vT)


def _glu(u):
    return u[:, :CONV_W] * jax.nn.sigmoid(u[:, CONV_W:])


def _zero_bits(x):
    bits = pltpu.bitcast(x, jnp.uint32)
    z = lax.shift_right_logical(lax.shift_right_logical(bits, jnp.uint32(16)), jnp.uint32(16))
    z = functools.reduce(
        jnp.bitwise_or, [z[r:r + SUBLANES] for r in range(0, x.shape[0], SUBLANES)])
    return pltpu.bitcast(z, jnp.int32).astype(F32)


def _add_token(x, token):
    r, c = token.shape
    top = x[0:r, 0:c] + token
    top = jnp.concatenate([top, x[0:r, c:]], axis=1) if c < x.shape[1] else top
    return jnp.concatenate([top, x[r:]], axis=0)


def _convpool_stages(uc_ref, uc_prev_ref, uc_next_ref, up_ref, up_prev_ref, up_next_ref,
                     has_prev, has_next, cnt_ref,
                     w_dw_ref, b_dw_ref, g_cn_ref, b_cn_ref,
                     hs_ref, hsh_ref, ps_ref, hn_ref, pd_ref):
    tm = uc_ref.shape[0]
    shift_rows = hsh_ref.shape[1]

    def halo_stage():
        if uc_prev_ref is None:
            hs_ref[0:HALO, :] = jnp.zeros((HALO, CONV_W), F32)
            ps_ref[0:HALO, :] = jnp.zeros((HALO, POOL_W), F32)
        else:
            hs_ref[0:HALO, :] = _glu(uc_prev_ref[...]) * has_prev
            ps_ref[0:HALO, :] = up_prev_ref[...] * has_prev
        hs_ref[HALO:HALO + tm, :] = _glu(uc_ref[...])
        hs_ref[HALO + tm:2 * HALO + tm, :] = _glu(uc_next_ref[...]) * has_next
        ps_ref[HALO:HALO + tm, :] = up_ref[...]
        ps_ref[HALO + tm:2 * HALO + tm, :] = up_next_ref[...] * has_next

    def shift_stage():
        for s in range(1, SUBLANES):
            hsh_ref[s - 1, 0:shift_rows, :] = hs_ref[s:s + shift_rows, :]

    def tap(c, k):
        off = HALO - CONV_PAD + k
        s, row = off % SUBLANES, c * CONV_CHUNK + off - off % SUBLANES
        if s == 0:
            return hs_ref[row:row + CONV_CHUNK, :]
        return hsh_ref[s - 1, row:row + CONV_CHUNK, :]

    def conv_stage(c):
        def run():
            groups = (CONV_CHUNK // SUBLANES, SUBLANES, CONV_W)
            acc = jnp.broadcast_to(b_dw_ref[...], groups)
            for k in range(CONV_KERNEL):
                acc = acc + tap(c, k).reshape(groups) * w_dw_ref[k]
            acc = acc.reshape(CONV_CHUNK, CONV_W)
            mu = jnp.mean(acc, axis=-1, keepdims=True)
            xc = acc - mu
            var = jnp.mean(xc * xc, axis=-1, keepdims=True)
            hn = xc * lax.rsqrt(var + LN_EPS) * g_cn_ref[...] + b_cn_ref[...]
            act = hn * jax.nn.sigmoid(hn)
            hn_ref[c * CONV_CHUNK:(c + 1) * CONV_CHUNK, :] = act.astype(BF16)
            return _zero_bits(act)
        return run

    def pool_stage():
        def shifted(d, lo, hi):
            return ps_ref[HALO + d:HALO + d + tm, lo:hi]

        lane = lax.broadcasted_iota(jnp.int32, (tm, LANES), 1)
        first = lane < POOL_GW
        s2 = shifted(-1, 0, LANES) + shifted(0, 0, LANES)
        s4 = s2 + shifted(-2, 0, LANES) + shifted(1, 0, LANES)
        mean_lo = jnp.where(first, s2, s4) / cnt_ref[:, 0:LANES]
        s8 = shifted(-4, LANES, 2 * LANES)
        for d in range(-3, 4):
            s8 = s8 + shifted(d, LANES, 2 * LANES)
        s16 = s8
        for d in (-8, -7, -6, -5, 4, 5, 6, 7):
            s16 = s16 + shifted(d, LANES, 2 * LANES)
        mean_hi = jnp.where(first, s8, s16) / cnt_ref[:, LANES:2 * LANES]
        d_lo = mean_lo - shifted(0, 0, LANES)
        d_hi = mean_hi - shifted(0, LANES, 2 * LANES)
        pd_ref[:, 0:LANES] = d_lo.astype(BF16)
        pd_ref[:, LANES:2 * LANES] = d_hi.astype(BF16)
        return _zero_bits(jnp.concatenate([d_lo, d_hi], axis=-1))

    return [halo_stage, pool_stage, shift_stage] + [conv_stage(c) for c in range(tm // CONV_CHUNK)]


def _convpool_outputs(hn_ref, pd_ref, w_pw_ref, w_pool_ref, pscale_ref):
    y_conv = jnp.dot(hn_ref[...], w_pw_ref[...], preferred_element_type=F32)
    y_pool = jnp.dot(pd_ref[...], w_pool_ref[...], preferred_element_type=F32) * pscale_ref[...]
    return y_conv.astype(BF16), y_pool.astype(BF16)


def _layer_norm(x, g, b):
    mu = jnp.mean(x, axis=-1, keepdims=True)
    xc = x - mu
    var = jnp.mean(xc * xc, axis=-1, keepdims=True)
    return xc * lax.rsqrt(var + LN_EPS) * g + b


def _mixffn_kernel(x0_ref, ya0_ref, x_next_ref, ya_next_ref,
                   uc0_ref, uc0_next_ref, up0_ref, up0_next_ref, cnt0_ref,
                   uc_ref, uc_prev_ref, uc_next_ref, up_ref, up_prev_ref, up_next_ref, cnt_ref,
                   w_dw_ref, b_dw_ref, g_cn_ref, b_cn_ref, w_pw_ref, w_pool_ref, pscale_ref,
                   w_out_ref, ln1_g_ref, ln1_b_ref, w_gate_ref, w_up_ref, w_down_ref,
                   ln2_g_ref, ln2_b_ref,
                   o_ref, hs_ref, hsh_ref, ps_ref, hn_ref, pd_ref, x1_ref, h_ref, *, seq_len):
    tm = o_ref.shape[0]
    tiles_per_seq = seq_len // tm
    i = pl.program_id(0)
    slot = i % 2
    vector_params = (w_dw_ref, b_dw_ref, g_cn_ref, b_cn_ref, hs_ref, hsh_ref, ps_ref, hn_ref, pd_ref)

    def mixed_norm(xr_ref, yar_ref):
        y_conv, y_pool = _convpool_outputs(hn_ref, pd_ref, w_pw_ref, w_pool_ref, pscale_ref)
        mix = jnp.dot(y_conv, w_out_ref[0:CONV_W, :], preferred_element_type=F32)
        mix += jnp.dot(y_pool, w_out_ref[CONV_W:CONV_W + POOL_W, :], preferred_element_type=F32)
        mix += jnp.dot(yar_ref[...], w_out_ref[CONV_W + POOL_W:, :], preferred_element_type=F32)
        return _layer_norm(ALPHA * xr_ref[...] + mix, ln1_g_ref[...], ln1_b_ref[...])

    @pl.when(i == 0)
    def _():
        for stage in _convpool_stages(uc0_ref, None, uc0_next_ref, up0_ref, None, up0_next_ref,
                                      0.0, 1.0 if tiles_per_seq > 1 else 0.0, cnt0_ref,
                                      *vector_params):
            stage()
        x1_ref[0] = mixed_norm(x0_ref, ya0_ref)

    x1b = x1_ref[slot].astype(BF16)

    j = jnp.minimum(i + 1, pl.num_programs(0) - 1) % tiles_per_seq
    next_stages = _convpool_stages(
        uc_ref, uc_prev_ref, uc_next_ref, up_ref, up_prev_ref, up_next_ref,
        (j > 0).astype(F32), (j < tiles_per_seq - 1).astype(F32), cnt_ref, *vector_params)
    n_chunks = D_FF // FF_CHUNK
    assert len(next_stages) <= n_chunks
    tokens = []
    for c in range(n_chunks):
        cols = slice(c * FF_CHUNK, (c + 1) * FF_CHUNK)
        g = jnp.dot(x1b, w_gate_ref[:, cols], preferred_element_type=F32)
        u = jnp.dot(x1b, w_up_ref[:, cols], preferred_element_type=F32)
        hc = g * jax.nn.sigmoid(g) * u
        token = tokens[c - TOKEN_LAG] if c >= TOKEN_LAG else None
        h_ref[:, cols] = (hc if token is None else _add_token(hc, token)).astype(BF16)
        tokens.append(next_stages[c]() if c < len(next_stages) else None)
    x1_ref[1 - slot] = mixed_norm(x_next_ref, ya_next_ref)
    y = jnp.dot(h_ref[...], w_down_ref[...], preferred_element_type=F32)
    o_ref[...] = _layer_norm(ALPHA * x1_ref[slot] + y, ln2_g_ref[...], ln2_b_ref[...])


def _mixffn_call(x, ya, uconv, upool, pool_cnt, seq_len, w_dw, b_dw, g_cn, b_cn, w_pw, w_pool_bd, pool_scale,
                 w_out, ln1_g, ln1_b, w_gate, w_up, w_down, ln2_g, ln2_b):
    T = x.shape[0]
    tm = TM_FFN
    n = T // tm
    r = tm // HALO
    n_halo_blocks = T // HALO

    def row(i):
        return (i, 0)

    def nxt_tile(i):
        return (jnp.minimum(i + 1, n - 1), 0)

    def nxt_tile_prev_halo(i):
        return (jnp.maximum(jnp.minimum(i + 1, n - 1) * r - 1, 0), 0)

    def nxt_tile_next_halo(i):
        return (jnp.minimum((jnp.minimum(i + 1, n - 1) + 1) * r, n_halo_blocks - 1), 0)

    once = pl.Buffered(1)

    def first_tile(i):
        return (0, 0)

    def first_tile_next_halo(i):
        return (r, 0)

    def nxt_tile_in_seq(i):
        return (jnp.minimum(i + 1, n - 1) % (seq_len // tm), 0)

    return pl.pallas_call(
        functools.partial(_mixffn_kernel, seq_len=seq_len),
        grid=(n,),
        in_specs=[
            pl.BlockSpec((tm, D_MODEL), first_tile, pipeline_mode=once),
            pl.BlockSpec((tm, ATTN_W), first_tile, pipeline_mode=once),
            pl.BlockSpec((tm, D_MODEL), nxt_tile),
            pl.BlockSpec((tm, ATTN_W), nxt_tile),
            pl.BlockSpec((tm, 2 * CONV_W), first_tile, pipeline_mode=once),
            pl.BlockSpec((HALO, 2 * CONV_W), first_tile_next_halo, pipeline_mode=once),
            pl.BlockSpec((tm, POOL_W), first_tile, pipeline_mode=once),
            pl.BlockSpec((HALO, POOL_W), first_tile_next_halo, pipeline_mode=once),
            pl.BlockSpec((tm, POOL_W), first_tile, pipeline_mode=once),
            pl.BlockSpec((tm, 2 * CONV_W), nxt_tile),
            pl.BlockSpec((HALO, 2 * CONV_W), nxt_tile_prev_halo),
            pl.BlockSpec((HALO, 2 * CONV_W), nxt_tile_next_halo),
            pl.BlockSpec((tm, POOL_W), nxt_tile),
            pl.BlockSpec((HALO, POOL_W), nxt_tile_prev_halo),
            pl.BlockSpec((HALO, POOL_W), nxt_tile_next_halo),
            pl.BlockSpec((tm, POOL_W), nxt_tile_in_seq),
            _const_spec((CONV_KERNEL, SUBLANES, CONV_W)),
            _const_spec((1, CONV_W)),
            _const_spec((1, CONV_W)),
            _const_spec((1, CONV_W)),
            _const_spec((CONV_W, CONV_W)),
            _const_spec((POOL_W, POOL_W)),
            _const_spec((1, POOL_W)),
            _const_spec((D_MODEL, D_MODEL)),
            _const_spec((1, D_MODEL)),
            _const_spec((1, D_MODEL)),
            _const_spec((D_MODEL, D_FF)),
            _const_spec((D_MODEL, D_FF)),
            _const_spec((D_FF, D_MODEL)),
            _const_spec((1, D_MODEL)),
            _const_spec((1, D_MODEL)),
        ],
        out_specs=pl.BlockSpec((tm, D_MODEL), row),
        out_shape=jax.ShapeDtypeStruct((T, D_MODEL), F32),
        scratch_shapes=[
            pltpu.VMEM((tm + 2 * HALO, CONV_W), F32),
            pltpu.VMEM((SUBLANES - 1, tm + 2 * HALO - SUBLANES, CONV_W), F32),
            pltpu.VMEM((tm + 2 * HALO, POOL_W), F32),
            pltpu.VMEM((tm, CONV_W), BF16),
            pltpu.VMEM((tm, POOL_W), BF16),
            pltpu.VMEM((2, tm, D_MODEL), F32),
            pltpu.VMEM((tm, D_FF), BF16),
        ],
        compiler_params=pltpu.CompilerParams(
            dimension_semantics=("arbitrary",), vmem_limit_bytes=VMEM_LIMIT),
        name="mixffn",
    )(x, ya, x, ya, uconv, uconv, upool, upool, pool_cnt, uconv, uconv, uconv, upool, upool, upool, pool_cnt,
      w_dw, b_dw, g_cn, b_cn, w_pw, w_pool_bd, pool_scale,
      w_out, ln1_g, ln1_b, w_gate, w_up, w_down, ln2_g, ln2_b)


def _rope_tables(S):
    pos = jnp.arange(S, dtype=F32)
    inv_freq = 1.0 / (ROPE_BASE ** (jnp.arange(0, QK_ROPE_DIM, 2, dtype=F32) / QK_ROPE_DIM))
    ang = pos[:, None] * inv_freq[None, :]
    cos, sin = jnp.cos(ang), jnp.sin(ang)
    zeros = jnp.zeros((S, LANES - QK_ROPE_DIM), F32)
    cosf = jnp.concatenate([cos, cos, zeros], axis=-1)
    sinf = jnp.concatenate([-sin, sin, zeros], axis=-1)
    return cosf, sinf, cos.T, sin.T


def _pool_counts(S):
    t = jnp.arange(S)
    cols = []
    for w in POOL_WINDOWS:
        left = w // 2
        right = w - 1 - left
        cnt = (jnp.minimum(t + right, S - 1) - jnp.maximum(t - left, 0) + 1).astype(F32)
        cols.append(jnp.broadcast_to(cnt[:, None], (S, POOL_GW)))
    return jnp.concatenate(cols, axis=-1)


def _prepare_layer(w_in, w_pw, w_pool, w_uq, w_ukv, w_out, w_gate, w_up, w_down):
    w_in_p = jnp.pad(w_in, ((0, 0), (0, D_IN_PAD - D_IN))).astype(BF16)
    wq = w_uq.reshape(Q_LORA_RANK, N_HEADS, QK_HEAD_DIM)
    w_uqT = jnp.concatenate(
        [wq[:, :, :QK_NOPE_DIM].reshape(Q_LORA_RANK, N_HEADS * QK_NOPE_DIM),
         wq[:, :, QK_NOPE_DIM:].reshape(Q_LORA_RANK, N_HEADS * QK_ROPE_DIM)], axis=-1).T.astype(BF16)
    wkv = w_ukv.reshape(KV_LORA_RANK, N_HEADS, QK_NOPE_DIM + V_HEAD_DIM)
    w_uk = wkv[:, :, :QK_NOPE_DIM].reshape(KV_LORA_RANK, N_HEADS * QK_NOPE_DIM).astype(BF16)
    w_uvT = wkv[:, :, QK_NOPE_DIM:].reshape(KV_LORA_RANK, N_HEADS * V_HEAD_DIM).T.astype(BF16)
    w_pool_bd = jnp.zeros((POOL_W, POOL_W), F32)
    for g in range(len(POOL_WINDOWS)):
        w_pool_bd = w_pool_bd.at[g * POOL_GW:(g + 1) * POOL_GW, g * POOL_GW:(g + 1) * POOL_GW].set(w_pool[g])
    return (w_in_p, w_pw.astype(BF16), w_pool_bd.astype(BF16), w_uqT, w_uk, w_uvT,
            w_out.astype(BF16), w_gate.astype(BF16), w_up.astype(BF16), w_down.astype(BF16))


def _trunk(x, layers, small):
    B, S, _ = x.shape
    cosf, sinf, cosT, sinT = _rope_tables(S)
    pool_cnt = _pool_counts(S)
    for l in range(DEPTH):
        w_in_p, w_pw_b, w_pool_bd, w_uqT, w_uk, w_uvT, w_out_b, w_gate_b, w_up_b, w_down_b = layers[l]
        (w_dw, b_dw, g_cn, b_cn, pool_scale, g_q, g_kv, ln1_g, ln1_b, ln2_g, ln2_b) = small[l]
        uconv, upool, qT, k, vT = _prep_call(
            x, w_in_p, g_q, g_kv, w_uqT, w_uk, w_uvT, cosf, sinf, cosT, sinT)
        yattn = _attn_call(qT, k, vT)
        x = _mixffn_call(
            x.reshape(B * S, D_MODEL), yattn.reshape(B * S, ATTN_W),
            uconv.reshape(B * S, 2 * CONV_W), upool.reshape(B * S, POOL_W), pool_cnt, S,
            w_dw, b_dw, g_cn, b_cn, w_pw_b, w_pool_bd, pool_scale,
            w_out_b, ln1_g, ln1_b, w_gate_b, w_up_b, w_down_b, ln2_g, ln2_b).reshape(B, S, D_MODEL)
    return x


def kernel(x_prompt, x_sample, w_in, w_dw, b_dw, g_cn, b_cn, w_pw, w_pool, pool_scale, g_q, g_kv,
           w_uq, w_ukv, w_out, ln1_g, ln1_b, w_gate, w_up, w_down, ln2_g, ln2_b):
    layers = [_prepare_layer(w_in[l], w_pw[l], w_pool[l], w_uq[l], w_ukv[l], w_out[l],
                             w_gate[l], w_up[l], w_down[l]) for l in range(DEPTH)]

    def row(a, l):
        return a[l].reshape(1, -1)

    def sublane_rows(a):
        return jnp.broadcast_to(a[:, None, :], (a.shape[0], SUBLANES, a.shape[1]))

    small = [(sublane_rows(w_dw[l]), row(b_dw, l), row(g_cn, l), row(b_cn, l), row(pool_scale, l), row(g_q, l),
              row(g_kv, l), row(ln1_g, l), row(ln1_b, l), row(ln2_g, l), row(ln2_b, l))
             for l in range(DEPTH)]
    y_prompt = _trunk(x_prompt, layers, small)
    y_sample = _trunk(x_sample, layers, small)
    return (y_prompt, y_sample)
```
